```python
import jax, jax.numpy as jnp
from jax import lax
import numpy as np

D_MODEL = 1024
BATCH = 1
SEQ = 16384
DEPTH = 2
DEC_BATCH = 8
DEC_SEQ = 8192
PAST_LEN = 128

MIX_WIDTH = D_MODEL
ATTN_WIDTH = MIX_WIDTH // 2
ATTN_HEAD_DIM = 64
ATTN_HEADS = ATTN_WIDTH // ATTN_HEAD_DIM
ROT_DIM = ATTN_HEAD_DIM // 4
ROPE_THETA = 500000.0
SEGMENTS = ((128, 1), (512, 4), (2048, 16))
MLSTM_WIDTH = MIX_WIDTH - ATTN_WIDTH
MLSTM_HEAD_DIM = 128
MLSTM_HEADS = MLSTM_WIDTH // MLSTM_HEAD_DIM
CONV_WIDTH = 5
CHUNK = 64
D_IN = 3 * ATTN_WIDTH + 4 * MLSTM_WIDTH + 4 * MLSTM_HEADS
N_EXPERTS = 32
TOP_K = 4
D_FF = D_MODEL
SWIGLU_LIMIT = 7.0
SWIGLU_ALPHA = 1.702
MOE_BLOCK = 256
RMS_EPS = 1e-6
NEG_INF = -1e30

kernel_name = "hymba_longnet_mlstm_moe_encoder"


def cdiv(a, b):
    return (a + b - 1) // b


def rms_norm(x, g):
    xf = x.astype(jnp.float32)
    y = xf * lax.rsqrt(jnp.mean(xf * xf, axis=-1, keepdims=True) + RMS_EPS)
    return (y * g.astype(jnp.float32)).astype(x.dtype)


def rotary(x, positions):
    inv_freq = ROPE_THETA ** (-jnp.arange(0, ROT_DIM, 2, dtype=jnp.float32) / ROT_DIM)
    ang = positions.astype(jnp.float32)[:, None] * inv_freq[None, :]
    cos = jnp.cos(ang)[None, :, None, :]
    sin = jnp.sin(ang)[None, :, None, :]
    half = ROT_DIM // 2
    x1, x2, xp = x[..., :half], x[..., half:ROT_DIM], x[..., ROT_DIM:]
    return jnp.concatenate([x1 * cos - x2 * sin, x2 * cos + x1 * sin, xp], axis=-1)


def to_sub(x, d):
    B, S = x.shape[:2]
    rest = x.shape[2:]
    x = x.reshape((B, S // d, d) + rest)
    return jnp.moveaxis(x, 2, 1).reshape((B * d, S // d) + rest)


def from_sub(x, B, d):
    N, L = x.shape[:2]
    rest = x.shape[2:]
    x = jnp.moveaxis(x.reshape((B, d, L) + rest), 1, 2)
    return x.reshape((B, L * d) + rest)


def dilated_window_attention(q, k, v, window, dilation):
    B, S, H, Dh = q.shape
    half = window // (2 * dilation)
    blk = half
    qs, ks, vs = to_sub(q, dilation), to_sub(k, dilation), to_sub(v, dilation)
    N, L = qs.shape[:2]
    nb = cdiv(L, blk)
    Lp = nb * blk
    qb = jnp.pad(qs, ((0, 0), (0, Lp - L), (0, 0), (0, 0))).reshape(N, nb, blk, H, Dh)

    def key_blocks(t):
        tp = jnp.pad(t, ((0, 0), (blk, Lp - L + blk), (0, 0), (0, 0))).reshape(N, nb + 2, blk, H, Dh)
        return jnp.concatenate([tp[:, :nb], tp[:, 1:nb + 1], tp[:, 2:]], axis=2)

    kb, vb = key_blocks(ks), key_blocks(vs)
    t_idx = jnp.arange(blk)[:, None]
    u_idx = jnp.arange(3 * blk)[None, :]
    band = jnp.abs(u_idx - blk - t_idx) <= half
    key_pos = jnp.arange(nb)[:, None] * blk - blk + jnp.arange(3 * blk)[None, :]
    valid = (key_pos >= 0) & (key_pos < L)
    mask = band[None, :, :] & valid[:, None, :]
    scores = jnp.einsum('nbqhd,nbkhd->nbhqk', qb, kb) * (Dh ** -0.5)
    scores = jnp.where(mask[None, :, None, :, :], scores, NEG_INF)
    mx = jnp.max(scores, axis=-1, keepdims=True)
    p = jnp.exp(scores - mx)
    den = jnp.sum(p, axis=-1)
    o = jnp.einsum('nbhqk,nbkhd->nbqhd', p, vb) / jnp.moveaxis(den, 2, 3)[..., None]
    lse = jnp.moveaxis(mx[..., 0] + jnp.log(den), 2, 3)
    o = o.reshape(N, Lp, H, Dh)[:, :L]
    lse = lse.reshape(N, Lp, H)[:, :L]
    return from_sub(o, B, dilation), from_sub(lse, B, dilation)


def mlstm_scan(q, k, v, log_i, log_f):
    B, S, H, DK = q.shape
    DV = v.shape[-1]
    nc = S // CHUNK

    def chunks(t):
        t = t.reshape((B, nc, CHUNK) + t.shape[2:])
        return jnp.moveaxis(jnp.moveaxis(t, 1, 0), 2, 3)

    qc, kc, vc = chunks(q), chunks(k), chunks(v)
    lic = jnp.moveaxis(log_i.reshape(B, nc, CHUNK, H), 1, 0).transpose(0, 1, 3, 2)
    lfc = jnp.moveaxis(log_f.reshape(B, nc, CHUNK, H), 1, 0).transpose(0, 1, 3, 2)
    tri = jnp.tril(jnp.ones((CHUNK, CHUNK), dtype=bool))

    def step(carry, inp):
        C, n, m = carry
        qq, kk, vv, li, lf = inp
        b = jnp.cumsum(lf, axis=-1)
        dmat = b[..., :, None] - b[..., None, :] + li[..., None, :]
        dmat = jnp.where(tri, dmat, NEG_INF)
        inter = b + m[..., None]
        m_t = jnp.maximum(inter, jnp.max(dmat, axis=-1))
        w_intra = jnp.exp(dmat - m_t[..., None])
        w_inter = jnp.exp(inter - m_t)
        s = jnp.einsum('bhtd,bhsd->bhts', qq, kk) * w_intra
        num = jnp.einsum('bhts,bhsv->bhtv', s, vv) + w_inter[..., None] * jnp.einsum('bhtd,bhdv->bhtv', qq, C)
        den = jnp.sum(s, axis=-1) + w_inter * jnp.einsum('bhtd,bhd->bht', qq, n)
        h = num / jnp.maximum(jnp.abs(den), jnp.exp(-m_t))[..., None]
        b_last = b[..., -1]
        log_w = b_last[..., None] - b + li
        m_new = jnp.maximum(b_last + m, jnp.max(log_w, axis=-1))
        ws = jnp.exp(log_w - m_new[..., None])
        keep = jnp.exp(b_last + m - m_new)
        C_new = keep[..., None, None] * C + jnp.einsum('bhs,bhsd,bhsv->bhdv', ws, kk, vv)
        n_new = keep[..., None] * n + jnp.einsum('bhs,bhsd->bhd', ws, kk)
        return (C_new, n_new, m_new), h

    init = (jnp.zeros((B, H, DK, DV), jnp.float32), jnp.zeros((B, H, DK), jnp.float32),
            jnp.zeros((B, H), jnp.float32))
    _, hs = lax.scan(step, init, (qc, kc, vc, lic, lfc))
    hs = jnp.moveaxis(jnp.moveaxis(hs, 0, 1), 2, 3)
    return hs.reshape(B, S, H, DV)


def centered_conv(x, w, b):
    pad = CONV_WIDTH // 2
    y = lax.conv_general_dilated(x, w[:, None, :].astype(x.dtype), window_strides=(1,),
                                 padding=[(pad, pad)], dimension_numbers=('NWC', 'WIO', 'NWC'),
                                 feature_group_count=x.shape[-1])
    return y + b.astype(x.dtype)


def mixer(h, w_in, q_norm_g, k_norm_g, attn_out_g, conv_w, conv_b, igate_b, fgate_b, mlstm_out_g, w_out):
    B, S, _ = h.shape
    z = h @ w_in.astype(h.dtype)
    aw, mw = ATTN_WIDTH, MLSTM_WIDTH
    q_a, k_a, v_a, qk_m, v_m, o_m, gates = jnp.split(
        z, [aw, 2 * aw, 3 * aw, 3 * aw + 2 * mw, 3 * aw + 3 * mw, 3 * aw + 4 * mw], axis=-1)

    pos = jnp.arange(S)
    qa = rotary(rms_norm(q_a.reshape(B, S, ATTN_HEADS, ATTN_HEAD_DIM), q_norm_g).astype(jnp.float32), pos)
    ka = rotary(rms_norm(k_a.reshape(B, S, ATTN_HEADS, ATTN_HEAD_DIM), k_norm_g).astype(jnp.float32), pos)
    va = v_a.reshape(B, S, ATTN_HEADS, ATTN_HEAD_DIM).astype(jnp.float32)
    outs, lses = [], []
    for window, dilation in SEGMENTS:
        o_g, l_g = dilated_window_attention(qa, ka, va, window, dilation)
        outs.append(o_g)
        lses.append(l_g)
    wts = jax.nn.softmax(jnp.stack(lses, axis=0), axis=0)
    attn = jnp.einsum('gbsh,gbshd->bshd', wts, jnp.stack(outs, axis=0))
    attn = rms_norm(attn, attn_out_g.reshape(ATTN_HEADS, ATTN_HEAD_DIM)).reshape(B, S, aw)

    qk = jax.nn.silu(centered_conv(qk_m, conv_w, conv_b)).astype(jnp.float32)
    qm = qk[..., :mw].reshape(B, S, MLSTM_HEADS, MLSTM_HEAD_DIM)
    km = qk[..., mw:].reshape(B, S, MLSTM_HEADS, MLSTM_HEAD_DIM) * (MLSTM_HEAD_DIM ** -0.5)
    vm = v_m.astype(jnp.float32).reshape(B, S, MLSTM_HEADS, MLSTM_HEAD_DIM)
    g = gates.astype(jnp.float32)
    ig_f, ig_b, fg_f, fg_b = jnp.split(g, 4, axis=-1)
    ib = igate_b.astype(jnp.float32)
    fb = fgate_b.astype(jnp.float32)
    h_fwd = mlstm_scan(qm, km, vm, ig_f + ib[0], jax.nn.log_sigmoid(fg_f + fb[0]))
    flip = lambda t: jnp.flip(t, axis=1)
    h_bwd = flip(mlstm_scan(flip(qm), flip(km), flip(vm), flip(ig_b + ib[1]),
                            flip(jax.nn.log_sigmoid(fg_b + fb[1]))))
    o_gate = jax.nn.sigmoid(o_m.astype(jnp.float32)).reshape(B, S, MLSTM_HEADS, MLSTM_HEAD_DIM)
    ml = o_gate * (h_fwd + h_bwd)
    ml = rms_norm(ml, mlstm_out_g.reshape(MLSTM_HEADS, MLSTM_HEAD_DIM)).reshape(B, S, mw)

    mixed = jnp.concatenate([attn, ml], axis=-1).astype(h.dtype)
    return mixed @ w_out.astype(h.dtype)


def moe(h, router_w, router_b, w_up, b_up, w_down, b_down):
    B, S, D = h.shape
    x2d = h.reshape(B * S, D)
    T = B * S
    logits = (x2d @ router_w.astype(h.dtype) + router_b.astype(h.dtype)).astype(jnp.float32)
    top_val, top_idx = lax.top_k(logits, TOP_K)
    gates = jax.nn.softmax(top_val, axis=-1)
    A = T * TOP_K
    flat_e = top_idx.reshape(-1)
    order = jnp.argsort(flat_e)
    counts = jnp.bincount(flat_e, length=N_EXPERTS)
    padded = cdiv(counts, MOE_BLOCK) * MOE_BLOCK
    pad_end = jnp.cumsum(padded)
    pad_start = pad_end - padded
    start = jnp.cumsum(counts) - counts
    sorted_e = flat_e[order]
    dest_sorted = pad_start[sorted_e] + (jnp.arange(A) - start[sorted_e])
    dest = jnp.zeros((A,), jnp.int32).at[order].set(dest_sorted.astype(jnp.int32))
    n_blocks = cdiv(A + N_EXPERTS * (MOE_BLOCK - 1), MOE_BLOCK)
    n_pad = n_blocks * MOE_BLOCK
    tok = (jnp.arange(A) // TOP_K).astype(jnp.int32)
    src = jnp.zeros((n_pad,), jnp.int32).at[dest].set(tok)
    x_pad = x2d[src].reshape(n_blocks, MOE_BLOCK, D)
    block_expert = jnp.clip(jnp.searchsorted(pad_end, jnp.arange(n_blocks) * MOE_BLOCK, side='right'),
                            0, N_EXPERTS - 1)

    def expert_block(args):
        xb, e = args
        hcat = xb @ w_up[e].astype(xb.dtype) + b_up[e].astype(xb.dtype)
        x_glu, x_lin = hcat[:, :D_FF], hcat[:, D_FF:]
        x_glu = jnp.minimum(x_glu, SWIGLU_LIMIT)
        x_lin = jnp.clip(x_lin, -SWIGLU_LIMIT, SWIGLU_LIMIT)
        act = x_glu * jax.nn.sigmoid(SWIGLU_ALPHA * x_glu) * (x_lin + 1.0)
        return act @ w_down[e].astype(xb.dtype) + b_down[e].astype(xb.dtype)

    y_pad = lax.map(expert_block, (x_pad, block_expert)).reshape(n_pad, D)
    y = y_pad[dest].reshape(T, TOP_K, D)
    out = jnp.einsum('tkd,tk->td', y, gates.astype(y.dtype))
    return out.reshape(B, S, D)


def trunk(x, norm1_g, w_in, q_norm_g, k_norm_g, attn_out_g, conv_w, conv_b, igate_b, fgate_b,
          mlstm_out_g, w_out, norm2_g, router_w, router_b, w_up, b_up, w_down, b_down):
    for l in range(DEPTH):
        x = x + mixer(rms_norm(x, norm1_g[l]), w_in[l], q_norm_g[l], k_norm_g[l], attn_out_g[l],
                      conv_w[l], conv_b[l], igate_b[l], fgate_b[l], mlstm_out_g[l], w_out[l])
        x = x + moe(rms_norm(x, norm2_g[l]), router_w[l], router_b[l], w_up[l], b_up[l], w_down[l], b_down[l])
    return x


def setup_inputs(seed: int = 0) -> dict:
    key = jax.random.key(seed)
    ks = jax.random.split(key, 20)
    f32 = jnp.float32

    def nrm(k, shape, scale):
        return jax.random.normal(k, shape, f32) * scale

    return {
        'x_prompt': nrm(ks[0], (BATCH, SEQ, D_MODEL), 1.0),
        'x_sample': nrm(ks[1], (DEC_BATCH, DEC_SEQ, D_MODEL), 1.0),
        'norm1_g': 1.0 + nrm(ks[2], (DEPTH, D_MODEL), 0.02),
        'w_in': nrm(ks[3], (DEPTH, D_MODEL, D_IN), D_MODEL ** -0.5),
        'q_norm_g': 1.0 + nrm(ks[4], (DEPTH, ATTN_HEAD_DIM), 0.02),
        'k_norm_g': 1.0 + nrm(ks[5], (DEPTH, ATTN_HEAD_DIM), 0.02),
        'attn_out_g': 1.0 + nrm(ks[6], (DEPTH, ATTN_WIDTH), 0.02),
        'conv_w': nrm(ks[7], (DEPTH, CONV_WIDTH, 2 * MLSTM_WIDTH), CONV_WIDTH ** -0.5),
        'conv_b': nrm(ks[8], (DEPTH, 2 * MLSTM_WIDTH), 0.02),
        'igate_b': -1.0 + nrm(ks[9], (DEPTH, 2, MLSTM_HEADS), 0.1),
        'fgate_b': jnp.linspace(3.0, 6.0, MLSTM_HEADS, dtype=f32) + nrm(ks[10], (DEPTH, 2, MLSTM_HEADS), 0.1),
        'mlstm_out_g': 1.0 + nrm(ks[11], (DEPTH, MLSTM_WIDTH), 0.02),
        'w_out': nrm(ks[12], (DEPTH, MIX_WIDTH, D_MODEL), MIX_WIDTH ** -0.5),
        'norm2_g': 1.0 + nrm(ks[13], (DEPTH, D_MODEL), 0.02),
        'router_w': nrm(ks[14], (DEPTH, D_MODEL, N_EXPERTS), D_MODEL ** -0.5),
        'router_b': nrm(ks[15], (DEPTH, N_EXPERTS), 0.01),
        'w_up': nrm(ks[16], (DEPTH, N_EXPERTS, D_MODEL, 2 * D_FF), D_MODEL ** -0.5),
        'b_up': nrm(ks[17], (DEPTH, N_EXPERTS, 2 * D_FF), 0.02),
        'w_down': nrm(ks[18], (DEPTH, N_EXPERTS, D_FF, D_MODEL), D_FF ** -0.5),
        'b_down': nrm(ks[19], (DEPTH, N_EXPERTS, D_MODEL), 0.02),
    }


def reference(x_prompt, x_sample, norm1_g, w_in, q_norm_g, k_norm_g, attn_out_g, conv_w, conv_b,
              igate_b, fgate_b, mlstm_out_g, w_out, norm2_g, router_w, router_b, w_up, b_up, w_down, b_down):
    y_prompt = trunk(x_prompt, norm1_g, w_in, q_norm_g, k_norm_g, attn_out_g, conv_w, conv_b, igate_b,
                     fgate_b, mlstm_out_g, w_out, norm2_g, router_w, router_b, w_up, b_up, w_down, b_down)
    y_sample = trunk(x_sample, norm1_g, w_in, q_norm_g, k_norm_g, attn_out_g, conv_w, conv_b, igate_b,
                     fgate_b, mlstm_out_g, w_out, norm2_g, router_w, router_b, w_up, b_up, w_down, b_down)
    return (y_prompt, y_sample)
```

```python
import functools

import jax
import jax.numpy as jnp
from jax import lax
from jax.experimental import pallas as pl
from jax.experimental.pallas import tpu as pltpu

D_MODEL = 1024
DEPTH = 2
ATTN_WIDTH = 512
ATTN_HEAD_DIM = 64
ATTN_HEADS = 8
ROT_DIM = 16
ROPE_THETA = 500000.0
SEGMENTS = ((128, 1), (512, 4), (2048, 16))
HALO = 64
MLSTM_WIDTH = 512
MLSTM_HEAD_DIM = 128
MLSTM_HEADS = 4
CONV_WIDTH = 5
CHUNK = 64
N_EXPERTS = 32
TOP_K = 4
D_FF = 1024
SWIGLU_LIMIT = 7.0
SWIGLU_ALPHA = 1.702
RMS_EPS = 1e-6
NEG_INF = -1e30

LANES = 128
SUBLANES = 8
VMEM_LIMIT = 48 * 1024 * 1024

F32 = jnp.float32
BF16 = jnp.bfloat16
HIGHEST = lax.Precision.HIGHEST
NT_DIMS = (((1,), (1,)), ((), ()))
TN_DIMS = (((0,), (0,)), ((), ()))


def _params(*sem):
    return pltpu.CompilerParams(dimension_semantics=sem, vmem_limit_bytes=VMEM_LIMIT)


def _row_tile(n, target):
    t = min(n, target)
    assert n % t == 0, (n, t)
    return t


def _group_sumsq(z, bd_ref):
    z2 = z * z
    hi = z2.astype(BF16)
    lo = (z2 - hi.astype(F32)).astype(BF16)
    bd = bd_ref[...]
    return jnp.dot(hi, bd, preferred_element_type=F32) + jnp.dot(lo, bd, preferred_element_type=F32)


def _inproj_kernel(x_ref, g1_ref, w_ref, wg_ref, qg_ref, kg_ref, cos_ref, sinb_ref, sinc_ref, bd_ref,
                   qa_ref, ka_ref, va_ref, qkm_ref, vm_ref, om_ref, gt_ref):
    x = x_ref[...]
    xn = x * lax.rsqrt(jnp.mean(x * x, axis=-1, keepdims=True) + RMS_EPS) * g1_ref[...]
    xb = xn.astype(BF16)
    aw, mw = ATTN_WIDTH, MLSTM_WIDTH

    def proj(lo, hi):
        return jnp.dot(xb, w_ref[:, lo:hi], preferred_element_type=F32)

    reps = aw // LANES
    cos_a = jnp.concatenate([cos_ref[...]] * reps, axis=1)
    sin_b = jnp.concatenate([sinb_ref[...]] * reps, axis=1)
    sin_c = jnp.concatenate([sinc_ref[...]] * reps, axis=1)

    def norm_rot(z, g_ref):
        ss = _group_sumsq(z, bd_ref)
        y = z * lax.rsqrt(ss * (1.0 / ATTN_HEAD_DIM) + RMS_EPS) * g_ref[...]
        half = ROT_DIM // 2
        y_up = pltpu.roll(y, aw - half, axis=1)
        y_dn = pltpu.roll(y, half, axis=1)
        return y * cos_a + y_up * sin_b + y_dn * sin_c

    qa_ref[...] = norm_rot(proj(0, aw), qg_ref).astype(BF16)
    ka_ref[...] = norm_rot(proj(aw, 2 * aw), kg_ref).astype(BF16)
    va_ref[...] = proj(2 * aw, 3 * aw).astype(BF16)
    qkm_ref[...] = proj(3 * aw, 3 * aw + 2 * mw)
    vm_ref[...] = proj(3 * aw + 2 * mw, 3 * aw + 3 * mw).astype(BF16)
    om_ref[...] = proj(3 * aw + 3 * mw, 3 * aw + 4 * mw)
    gt_ref[...] = jnp.dot(xn, wg_ref[...], precision=HIGHEST, preferred_element_type=F32)


def _inproj(x2, g1, w_main, w_gate, qg, kg, cos_a, sin_b, sin_c, bd, seq):
    T = x2.shape[0]
    tm = _row_tile(seq, 512)
    nseq = seq // tm
    aw, mw = ATTN_WIDTH, MLSTM_WIDTH
    ngate = w_gate.shape[1]
    row = lambda w: pl.BlockSpec((tm, w), lambda i: (i, 0))
    const = lambda a: pl.BlockSpec(a.shape, lambda i: (0, 0))
    tab = pl.BlockSpec((tm, LANES), lambda i: (i % nseq, 0))
    return pl.pallas_call(
        _inproj_kernel,
        grid=(T // tm,),
        in_specs=[row(D_MODEL), const(g1), const(w_main), const(w_gate), const(qg), const(kg),
                  tab, tab, tab, const(bd)],
        out_specs=[row(aw), row(aw), row(aw), row(2 * mw), row(mw), row(mw), row(ngate)],
        out_shape=[jax.ShapeDtypeStruct((T, aw), BF16), jax.ShapeDtypeStruct((T, aw), BF16),
                   jax.ShapeDtypeStruct((T, aw), BF16), jax.ShapeDtypeStruct((T, 2 * mw), F32),
                   jax.ShapeDtypeStruct((T, mw), BF16), jax.ShapeDtypeStruct((T, mw), F32),
                   jax.ShapeDtypeStruct((T, ngate), F32)],
        compiler_params=_params("parallel"),
        name="inproj",
    )(x2, g1, w_main, w_gate, qg, kg, cos_a, sin_b, sin_c, bd)


def _attn_kernel(q_ref, kl_ref, kc_ref, kr_ref, vl_ref, vc_ref, vr_ref, o_ref, lse_ref, *, lq, sub_len):
    i = pl.program_id(2)
    lk = lq + 2 * HALO
    q = q_ref[0]
    k = jnp.concatenate([kl_ref[0], kc_ref[0], kr_ref[0]], axis=0)
    v = jnp.concatenate([vl_ref[0], vc_ref[0], vr_ref[0]], axis=0)

    row = lax.broadcasted_iota(jnp.int32, (2 * lq, lk), 0)
    col = lax.broadcasted_iota(jnp.int32, (2 * lq, lk), 1)
    qrow = jnp.where(row >= lq, row - lq, row)
    rel = col - HALO - qrow
    kpos = i * lq - HALO + col
    mask = (jnp.abs(rel) <= HALO) & (kpos >= 0) & (kpos < sub_len)

    lane2 = lax.broadcasted_iota(jnp.int32, (2 * lq, LANES), 1)
    row2 = lax.broadcasted_iota(jnp.int32, (2 * lq, LANES), 0)
    own = (lane2 >= ATTN_HEAD_DIM) ^ (row2 < lq)
    lane = lax.broadcasted_iota(jnp.int32, (lq, LANES), 1)
    first = lane < ATTN_HEAD_DIM

    for hp in range(ATTN_WIDTH // LANES):
        sl = slice(hp * LANES, (hp + 1) * LANES)
        qp = q[:, sl]
        q2 = jnp.concatenate([qp, qp], axis=0)
        q2 = jnp.where(own, q2, jnp.zeros_like(q2))
        s = lax.dot_general(q2, k[:, sl], NT_DIMS, preferred_element_type=F32) * (ATTN_HEAD_DIM ** -0.5)
        s = jnp.where(mask, s, NEG_INF)
        mx = jnp.max(s, axis=-1, keepdims=True)
        p = jnp.exp(s - mx)
        den = jnp.sum(p, axis=-1, keepdims=True)
        pv = jnp.dot(p.astype(BF16), v[:, sl], preferred_element_type=F32)
        o2 = pv / den
        lse2 = mx + jnp.log(den)
        o_ref[0, :, sl] = jnp.where(first, o2[:lq], o2[lq:])
        lse_ref[0, :, sl] = jnp.where(first, jnp.broadcast_to(lse2[:lq], (lq, LANES)),
                                      jnp.broadcast_to(lse2[lq:], (lq, LANES)))


def _attn_segment(qa, ka, va, batch, seq, dilation):
    aw = ATTN_WIDTH
    sub_len = seq // dilation
    lq = _row_tile(sub_len, 128)
    nq = sub_len // lq
    hb = lq // HALO
    nh = sub_len // HALO
    view = lambda a: a.reshape(batch, sub_len, dilation * aw)
    cen = pl.BlockSpec((1, lq, aw), lambda b, r, i: (b, i, r))
    left = pl.BlockSpec((1, HALO, aw), lambda b, r, i: (b, jnp.maximum(i * hb - 1, 0), r))
    right = pl.BlockSpec((1, HALO, aw), lambda b, r, i: (b, jnp.minimum((i + 1) * hb, nh - 1), r))
    o, lse = pl.pallas_call(
        functools.partial(_attn_kernel, lq=lq, sub_len=sub_len),
        grid=(batch, dilation, nq),
        in_specs=[cen, left, cen, right, left, cen, right],
        out_specs=[cen, cen],
        out_shape=[jax.ShapeDtypeStruct((batch, sub_len, dilation * aw), F32)] * 2,
        compiler_params=_params("parallel", "parallel", "parallel"),
        name=f"attn_d{dilation}",
    )(view(qa), view(ka), view(ka), view(ka), view(va), view(va), view(va))
    return o.reshape(batch * seq, aw), lse.reshape(batch * seq, aw)


def _log_sigmoid(x):
    return jnp.minimum(x, 0.0) - jnp.log(1.0 + jnp.exp(-jnp.abs(x)))


def _mlstm_kernel(*refs, tb, reverse, first_pass):
    nh, hd = MLSTM_HEADS, MLSTM_HEAD_DIM
    mw = MLSTM_WIDTH
    if first_pass:
        (qk_ref, qkp_ref, qkn_ref, cw_ref, cb_ref, v_ref, gc_ref, gr_ref, bc_ref, br_ref,
         h_ref, qo_ref, ko_ref, c_scr, m_scr, q_scr, k_scr, xe_scr) = refs
    else:
        (q_ref, k_ref, v_ref, gc_ref, gr_ref, bc_ref, br_ref, hf_ref, om_ref, og_ref,
         h_ref, c_scr, m_scr) = refs
    t = pl.program_id(1)
    nt = pl.num_programs(1)

    @pl.when(t == 0)
    def _():
        c_scr[...] = jnp.zeros_like(c_scr)
        m_scr[...] = jnp.zeros_like(m_scr)

    if first_pass:
        pad = CONV_WIDTH // 2
        blk = t if not reverse else nt - 1 - t
        prev = jnp.where(blk > 0, qkp_ref[0], 0.0)
        nxt = jnp.where(blk < nt - 1, qkn_ref[0], 0.0)
        xe_scr[0:SUBLANES, :] = prev
        xe_scr[SUBLANES:SUBLANES + tb, :] = qk_ref[0]
        xe_scr[SUBLANES + tb:, :] = nxt
        acc = jnp.zeros((tb, 2 * mw), F32) + cb_ref[...]
        for j in range(CONV_WIDTH):
            acc = acc + xe_scr[SUBLANES - pad + j:SUBLANES - pad + j + tb, :] * cw_ref[j:j + 1, :]
        act = acc * jax.nn.sigmoid(acc)
        qb = act[:, :mw].astype(BF16)
        kb = (act[:, mw:] * (hd ** -0.5)).astype(BF16)
        qo_ref[0] = qb
        ko_ref[0] = kb
        q_scr[...] = qb
        k_scr[...] = kb
        q_src, k_src = q_scr, k_scr
        rd = lambda ref, r0: ref[pl.ds(r0, CHUNK), :]
    else:
        q_src, k_src = q_ref, k_ref
        rd = lambda ref, r0: ref[0, pl.ds(r0, CHUNK), :]

    ti = lax.broadcasted_iota(jnp.int32, (CHUNK, CHUNK), 0)
    si = lax.broadcasted_iota(jnp.int32, (CHUNK, CHUNK), 1)
    causal = (si >= ti) if reverse else (si <= ti)
    cum_col = causal.astype(F32)
    cum_row = ((ti >= si) if reverse else (ti <= si)).astype(F32)
    last = 0 if reverse else CHUNK - 1
    ones_col = (lax.broadcasted_iota(jnp.int32, (CHUNK, hd), 1) == 0).astype(BF16)
    ngate = 2 * nh
    gc_is_i = lax.broadcasted_iota(jnp.int32, (CHUNK, ngate), 1) < nh
    gr_is_i = lax.broadcasted_iota(jnp.int32, (ngate, CHUNK), 0) < nh
    nchunk = tb // CHUNK

    def chunk_body(j, carry):
        jj = nchunk - 1 - j if reverse else j
        r0 = pl.multiple_of(jj * CHUNK, CHUNK)
        a_c = gc_ref[0, pl.ds(r0, CHUNK), :] + bc_ref[...]
        a_r = gr_ref[0, jj] + br_ref[...]
        gl_c = jnp.where(gc_is_i, a_c, _log_sigmoid(a_c))
        gl_r = jnp.where(gr_is_i, a_r, _log_sigmoid(a_r))
        b_c = jnp.dot(cum_col, gl_c, precision=HIGHEST, preferred_element_type=F32)
        b_r = jnp.dot(gl_r, cum_row, precision=HIGHEST, preferred_element_type=F32)
        qc = rd(q_src, r0)
        kc = rd(k_src, r0)
        vc = v_ref[0, pl.ds(r0, CHUNK), :]
        if not first_pass:
            hfc = hf_ref[0, pl.ds(r0, CHUNK), :]
            omc = om_ref[0, pl.ds(r0, CHUNK), :]
        for h in range(nh):
            hs = slice(h * hd, (h + 1) * hd)
            bc = b_c[:, nh + h:nh + h + 1]
            br = b_r[nh + h:nh + h + 1, :]
            lir = gl_r[h:h + 1, :]
            lic = gl_c[:, h:h + 1]
            m_prev = m_scr[h:h + 1, 0:1]
            dmat = jnp.where(causal, bc - br + lir, NEG_INF)
            inter = bc + m_prev
            m_t = jnp.maximum(inter, jnp.max(dmat, axis=-1, keepdims=True))
            w_intra = jnp.exp(dmat - m_t)
            w_inter = jnp.exp(inter - m_t)
            qh, kh, vh = qc[:, hs], kc[:, hs], vc[:, hs]
            s = lax.dot_general(qh, kh, NT_DIMS, preferred_element_type=F32) * w_intra
            v_aug = jnp.concatenate([vh, ones_col], axis=1)
            c_old = c_scr[h]
            h_aug = (jnp.dot(s.astype(BF16), v_aug, preferred_element_type=F32)
                     + w_inter * jnp.dot(qh, c_old.astype(BF16), preferred_element_type=F32))
            num = h_aug[:, :hd]
            den = h_aug[:, hd:hd + 1]
            h_out = num / jnp.maximum(jnp.abs(den), jnp.exp(-m_t))
            b_last = bc[last:last + 1, :]
            log_w = b_last - bc + lic
            m_new = jnp.maximum(b_last + m_prev, jnp.max(log_w, axis=0, keepdims=True))
            ws = jnp.exp(log_w - m_new)
            keep = jnp.exp(b_last + m_prev - m_new)
            wv = (ws * v_aug.astype(F32)).astype(BF16)
            c_scr[h] = keep * c_old + lax.dot_general(kh, wv, TN_DIMS, preferred_element_type=F32)
            m_scr[h:h + 1, :] = jnp.broadcast_to(m_new, (1, LANES))
            if first_pass:
                h_ref[0, pl.ds(r0, CHUNK), hs] = h_out
            else:
                y = jax.nn.sigmoid(omc[:, hs]) * (hfc[:, hs] + h_out)
                y = y * lax.rsqrt(jnp.mean(y * y, axis=-1, keepdims=True) + RMS_EPS) * og_ref[:, hs]
                h_ref[0, pl.ds(r0, CHUNK), hs] = y
        return carry

    lax.fori_loop(0, nchunk, chunk_body, 0)


def _mlstm_pass(batch, seq, reverse, first_pass, operands):
    mw, nh, hd = MLSTM_WIDTH, MLSTM_HEADS, MLSTM_HEAD_DIM
    tb = _row_tile(seq, 512)
    nt = seq // tb
    ngate = 2 * nh
    tidx = (lambda t: nt - 1 - t) if reverse else (lambda t: t)
    rows = lambda w: pl.BlockSpec((1, tb, w), lambda b, t: (b, tidx(t), 0))
    const = lambda a: pl.BlockSpec(a.shape, lambda b, t: (0,) * a.ndim)
    gate_rows = pl.BlockSpec((1, tb // CHUNK, ngate, CHUNK), lambda b, t: (b, tidx(t), 0, 0))
    hb = tb // SUBLANES
    nhb = seq // SUBLANES
    state = [pltpu.VMEM((nh, hd, 2 * hd), F32), pltpu.VMEM((SUBLANES, LANES), F32)]
    if first_pass:
        qk, cw, cb, v, gc, gr, bc, br = operands
        prev = pl.BlockSpec((1, SUBLANES, 2 * mw), lambda b, t: (b, jnp.maximum(tidx(t) * hb - 1, 0), 0))
        nxt = pl.BlockSpec((1, SUBLANES, 2 * mw), lambda b, t: (b, jnp.minimum((tidx(t) + 1) * hb, nhb - 1), 0))
        in_specs = [rows(2 * mw), prev, nxt, const(cw), const(cb), rows(mw), rows(ngate), gate_rows,
                    const(bc), const(br)]
        args = (qk, qk, qk, cw, cb, v, gc, gr, bc, br)
        out_specs = [rows(mw), rows(mw), rows(mw)]
        out_shape = [jax.ShapeDtypeStruct((batch, seq, mw), F32),
                     jax.ShapeDtypeStruct((batch, seq, mw), BF16),
                     jax.ShapeDtypeStruct((batch, seq, mw), BF16)]
        scratch = state + [pltpu.VMEM((tb, mw), BF16), pltpu.VMEM((tb, mw), BF16),
                           pltpu.VMEM((tb + 2 * SUBLANES, 2 * mw), F32)]
    else:
        q, k, v, gc, gr, bc, br, hf, om, og = operands
        in_specs = [rows(mw), rows(mw), rows(mw), rows(ngate), gate_rows, const(bc), const(br),
                    rows(mw), rows(mw), const(og)]
        args = operands
        out_specs = rows(mw)
        out_shape = jax.ShapeDtypeStruct((batch, seq, mw), F32)
        scratch = state
    return pl.pallas_call(
        functools.partial(_mlstm_kernel, tb=tb, reverse=reverse, first_pass=first_pass),
        grid=(batch, nt),
        in_specs=in_specs,
        out_specs=out_specs,
        out_shape=out_shape,
        scratch_shapes=scratch,
        compiler_params=_params("parallel", "arbitrary"),
        name="mlstm_fwd" if first_pass else "mlstm_bwd",
    )(*args)


def _outproj_kernel(o1_ref, o2_ref, o3_ref, l1_ref, l2_ref, l3_ref, ml_ref, x_ref, ag_ref, bd_ref, w_ref,
                    y_ref):
    l1, l2, l3 = l1_ref[...], l2_ref[...], l3_ref[...]
    mx = jnp.maximum(jnp.maximum(l1, l2), l3)
    e1, e2, e3 = jnp.exp(l1 - mx), jnp.exp(l2 - mx), jnp.exp(l3 - mx)
    attn = (e1 * o1_ref[...] + e2 * o2_ref[...] + e3 * o3_ref[...]) / (e1 + e2 + e3)
    ss = _group_sumsq(attn, bd_ref)
    attn = attn * lax.rsqrt(ss * (1.0 / ATTN_HEAD_DIM) + RMS_EPS) * ag_ref[...]
    aw = ATTN_WIDTH
    y = jnp.dot(attn.astype(BF16), w_ref[:aw, :], preferred_element_type=F32)
    y = y + jnp.dot(ml_ref[...].astype(BF16), w_ref[aw:, :], preferred_element_type=F32)
    y_ref[...] = x_ref[...] + y


def _outproj(os_, ls_, ml, x2, ag, bd, w_out):
    T = x2.shape[0]
    tm = _row_tile(T, 512)
    row = lambda w: pl.BlockSpec((tm, w), lambda i: (i, 0))
    const = lambda a: pl.BlockSpec(a.shape, lambda i: (0, 0))
    aw = ATTN_WIDTH
    return pl.pallas_call(
        _outproj_kernel,
        grid=(T // tm,),
        in_specs=[row(aw)] * 6 + [row(MLSTM_WIDTH), row(D_MODEL), const(ag), const(bd), const(w_out)],
        out_specs=row(D_MODEL),
        out_shape=jax.ShapeDtypeStruct((T, D_MODEL), F32),
        compiler_params=_params("parallel"),
        name="outproj",
    )(*os_, *ls_, ml, x2, ag, bd, w_out)


def _router_kernel(x_ref, g_ref, w_ref, b_ref, h_ref, lg_ref):
    x = x_ref[...]
    h = x * lax.rsqrt(jnp.mean(x * x, axis=-1, keepdims=True) + RMS_EPS) * g_ref[...]
    h_ref[...] = h.astype(BF16)
    lg_ref[...] = jnp.dot(h, w_ref[...], precision=HIGHEST, preferred_element_type=F32) + b_ref[...]


def _router(x2, g2, rw, rb):
    T = x2.shape[0]
    tm = _row_tile(T, 512)
    row = lambda w: pl.BlockSpec((tm, w), lambda i: (i, 0))
    const = lambda a: pl.BlockSpec(a.shape, lambda i: (0, 0))
    return pl.pallas_call(
        _router_kernel,
        grid=(T // tm,),
        in_specs=[row(D_MODEL), const(g2), const(rw), const(rb)],
        out_specs=[row(D_MODEL), row(N_EXPERTS)],
        out_shape=[jax.ShapeDtypeStruct((T, D_MODEL), BF16), jax.ShapeDtypeStruct((T, N_EXPERTS), F32)],
        compiler_params=_params("parallel"),
        name="router",
    )(x2, g2, rw, rb)


def _expert_kernel(be_ref, x_ref, wu_ref, bu_ref, wd_ref, bd_ref, y_ref):
    del be_ref
    hcat = jnp.dot(x_ref[...], wu_ref[0], preferred_element_type=F32) + bu_ref[0]
    x_glu = jnp.minimum(hcat[:, :D_FF], SWIGLU_LIMIT)
    x_lin = jnp.clip(hcat[:, D_FF:], -SWIGLU_LIMIT, SWIGLU_LIMIT)
    act = x_glu * jax.nn.sigmoid(SWIGLU_ALPHA * x_glu) * (x_lin + 1.0)
    y_ref[...] = jnp.dot(act.astype(BF16), wd_ref[0], preferred_element_type=F32) + bd_ref[0]


def _experts(block_expert, x_pad, wu, bu, wd, bd, bm):
    n_pad = x_pad.shape[0]
    grid_spec = pltpu.PrefetchScalarGridSpec(
        num_scalar_prefetch=1,
        grid=(n_pad // bm,),
        in_specs=[pl.BlockSpec((bm, D_MODEL), lambda i, be: (i, 0)),
                  pl.BlockSpec((1, D_MODEL, 2 * D_FF), lambda i, be: (be[i], 0, 0)),
                  pl.BlockSpec((1, 1, 2 * D_FF), lambda i, be: (be[i], 0, 0)),
                  pl.BlockSpec((1, D_FF, D_MODEL), lambda i, be: (be[i], 0, 0)),
                  pl.BlockSpec((1, 1, D_MODEL), lambda i, be: (be[i], 0, 0))],
        out_specs=pl.BlockSpec((bm, D_MODEL), lambda i, be: (i, 0)),
    )
    return pl.pallas_call(
        _expert_kernel,
        grid_spec=grid_spec,
        out_shape=jax.ShapeDtypeStruct((n_pad, D_MODEL), F32),
        compiler_params=_params("arbitrary"),
        name="experts",
    )(block_expert, x_pad, wu, bu, wd, bd)


def _combine_kernel(x_ref, y_ref, g_ref, o_ref):
    acc = x_ref[...]
    g = g_ref[...]
    for k in range(TOP_K):
        acc = acc + y_ref[:, k, :] * g[:, k:k + 1]
    o_ref[...] = acc


def _combine(x2, y4, gates):
    T = x2.shape[0]
    tm = _row_tile(T, 256)
    return pl.pallas_call(
        _combine_kernel,
        grid=(T // tm,),
        in_specs=[pl.BlockSpec((tm, D_MODEL), lambda i: (i, 0)),
                  pl.BlockSpec((tm, TOP_K, D_MODEL), lambda i: (i, 0, 0)),
                  pl.BlockSpec((tm, TOP_K), lambda i: (i, 0))],
        out_specs=pl.BlockSpec((tm, D_MODEL), lambda i: (i, 0)),
        out_shape=jax.ShapeDtypeStruct((T, D_MODEL), F32),
        compiler_params=_params("parallel"),
        name="combine",
    )(x2, y4, gates)


def _moe(x2, g2, rw, rb, wu, bu, wd, bd, bm=256):
    T = x2.shape[0]
    h, logits = _router(x2, g2, rw, rb)
    top_val, top_idx = lax.top_k(logits, TOP_K)
    gates = jax.nn.softmax(top_val, axis=-1)
    A = T * TOP_K
    flat_e = top_idx.reshape(-1)
    order = jnp.argsort(flat_e)
    counts = jnp.bincount(flat_e, length=N_EXPERTS)
    padded = ((counts + bm - 1) // bm) * bm
    pad_end = jnp.cumsum(padded)
    pad_start = pad_end - padded
    start = jnp.cumsum(counts) - counts
    sorted_e = flat_e[order]
    dest_sorted = pad_start[sorted_e] + (jnp.arange(A) - start[sorted_e])
    dest = jnp.zeros((A,), jnp.int32).at[order].set(dest_sorted.astype(jnp.int32))
    n_blocks = (A + N_EXPERTS * (bm - 1) + bm - 1) // bm
    n_pad = n_blocks * bm
    tok = (jnp.arange(A) // TOP_K).astype(jnp.int32)
    src = jnp.zeros((n_pad,), jnp.int32).at[dest].set(tok)
    x_pad = jnp.take(h, src, axis=0)
    block_expert = jnp.clip(jnp.searchsorted(pad_end, jnp.arange(n_blocks) * bm, side='right'),
                            0, N_EXPERTS - 1).astype(jnp.int32)
    y_pad = _experts(block_expert, x_pad, wu, bu, wd, bd, bm)
    y4 = jnp.take(y_pad, dest, axis=0).reshape(T, TOP_K, D_MODEL)
    return _combine(x2, y4, gates)


def _rotary_tables(seq):
    half = ROT_DIM // 2
    inv_freq = ROPE_THETA ** (-jnp.arange(0, ROT_DIM, 2, dtype=F32) / ROT_DIM)
    ang = jnp.arange(seq).astype(F32)[:, None] * inv_freq[None, :]
    cos, sin = jnp.cos(ang), jnp.sin(ang)
    rest = ATTN_HEAD_DIM - ROT_DIM
    ones = jnp.ones((seq, rest), F32)
    zeros = jnp.zeros((seq, rest), F32)
    zh = jnp.zeros((seq, half), F32)
    cos_a = jnp.concatenate([cos, cos, ones], axis=1)
    sin_b = jnp.concatenate([-sin, zh, zeros], axis=1)
    sin_c = jnp.concatenate([zh, sin, zeros], axis=1)
    tile = lambda a: jnp.tile(a, (1, LANES // ATTN_HEAD_DIM))
    return tile(cos_a), tile(sin_b), tile(sin_c)


def _layer_weights(l, norm1_g, w_in, q_norm_g, k_norm_g, attn_out_g, conv_w, conv_b, igate_b, fgate_b,
                   mlstm_out_g, w_out, norm2_g, router_w, router_b, w_up, b_up, w_down, b_down):
    aw, mw, nh = ATTN_WIDTH, MLSTM_WIDTH, MLSTM_HEADS
    n_main = 3 * aw + 4 * mw
    w = w_in[l]
    wg = w[:, n_main:]
    pick = lambda a, d: jnp.concatenate([a[..., d * nh:(d + 1) * nh],
                                         a[..., (2 + d) * nh:(3 + d) * nh]], axis=-1)
    gate_bias = [jnp.concatenate([igate_b[l, d], fgate_b[l, d]]) for d in range(2)]
    return dict(
        g1=norm1_g[l][None, :], w_main=w[:, :n_main].astype(BF16),
        w_gate=jnp.concatenate([pick(wg, 0), pick(wg, 1)], axis=1),
        qg=jnp.tile(q_norm_g[l], ATTN_HEADS)[None, :], kg=jnp.tile(k_norm_g[l], ATTN_HEADS)[None, :],
        ag=attn_out_g[l][None, :], cw=conv_w[l], cb=conv_b[l][None, :],
        bias_c=[b[None, :] for b in gate_bias], bias_r=[b[:, None] for b in gate_bias],
        og=mlstm_out_g[l][None, :], w_out=w_out[l].astype(BF16), g2=norm2_g[l][None, :],
        rw=router_w[l], rb=router_b[l][None, :],
        wu=w_up[l].astype(BF16), bu=b_up[l][:, None, :], wd=w_down[l].astype(BF16), bd=b_down[l][:, None, :])


def _trunk(x, layers, bd):
    batch, seq, _ = x.shape
    T = batch * seq
    mw, ngate = MLSTM_WIDTH, 2 * MLSTM_HEADS
    tables = _rotary_tables(seq)
    x2 = x.reshape(T, D_MODEL)
    for lw in layers:
        qa, ka, va, qkm, vm, om, gates = _inproj(x2, lw['g1'], lw['w_main'], lw['w_gate'], lw['qg'],
                                                 lw['kg'], *tables, bd, seq)
        segs = [_attn_segment(qa, ka, va, batch, seq, d) for _, d in SEGMENTS]
        gates = gates.reshape(batch, seq, 2 * ngate)
        gc = [gates[..., :ngate], gates[..., ngate:]]
        gr = [jnp.swapaxes(g.reshape(batch, seq // CHUNK, CHUNK, ngate), 2, 3) for g in gc]
        b3 = lambda a: a.reshape(batch, seq, a.shape[-1])
        hf, qm, km = _mlstm_pass(batch, seq, False, True,
                                 (b3(qkm), lw['cw'], lw['cb'], b3(vm), gc[0], gr[0],
                                  lw['bias_c'][0], lw['bias_r'][0]))
        ml = _mlstm_pass(batch, seq, True, False,
                         (qm, km, b3(vm), gc[1], gr[1], lw['bias_c'][1], lw['bias_r'][1],
                          hf, b3(om), lw['og']))
        x2 = _outproj([s[0] for s in segs], [s[1] for s in segs], ml.reshape(T, mw), x2,
                      lw['ag'], bd, lw['w_out'])
        x2 = _moe(x2, lw['g2'], lw['rw'], lw['rb'], lw['wu'], lw['bu'], lw['wd'], lw['bd'])
    return x2.reshape(batch, seq, D_MODEL)


def kernel(x_prompt, x_sample, norm1_g, w_in, q_norm_g, k_norm_g, attn_out_g, conv_w, conv_b, igate_b,
           fgate_b, mlstm_out_g, w_out, norm2_g, router_w, router_b, w_up, b_up, w_down, b_down):
    assert all(w // (2 * d) == HALO for w, d in SEGMENTS)
    weights = (norm1_g, w_in, q_norm_g, k_norm_g, attn_out_g, conv_w, conv_b, igate_b, fgate_b,
               mlstm_out_g, w_out, norm2_g, router_w, router_b, w_up, b_up, w_down, b_down)
    layers = [_layer_weights(l, *weights) for l in range(DEPTH)]
    idx = jnp.arange(ATTN_WIDTH) // ATTN_HEAD_DIM
    bd = (idx[:, None] == idx[None, :]).astype(BF16)
    return (_trunk(x_prompt, layers, bd), _trunk(x_sample, layers, bd))
```

```python
import functools

import jax
import jax.numpy as jnp
from jax import lax
from jax.experimental import pallas as pl
from jax.experimental.pallas import tpu as pltpu

D_MODEL = 1024
DEPTH = 2
ATTN_WIDTH = 512
ATTN_HEAD_DIM = 64
ATTN_HEADS = 8
ROT_DIM = 16
ROPE_THETA = 500000.0
SEGMENTS = ((128, 1), (512, 4), (2048, 16))
HALO = 64
MLSTM_WIDTH = 512
MLSTM_HEAD_DIM = 128
MLSTM_HEADS = 4
CONV_WIDTH = 5
CHUNK = 64
N_EXPERTS = 32
TOP_K = 4
D_FF = 1024
SWIGLU_LIMIT = 7.0
SWIGLU_ALPHA = 1.702
RMS_EPS = 1e-6
NEG_INF = -1e30

LANES = 128
SUBLANES = 8
TILE = 512
LQ = 128
EXPERT_ROWS = 512
VMEM_LIMIT = 48 * 1024 * 1024
VMEM_LIMIT_EXPERTS = 56 * 1024 * 1024

F32 = jnp.float32
BF16 = jnp.bfloat16
HIGHEST = lax.Precision.HIGHEST
NT_DIMS = (((1,), (1,)), ((), ()))
TN_DIMS = (((0,), (0,)), ((), ()))


def _params(*sem, vmem=VMEM_LIMIT):
    return pltpu.CompilerParams(dimension_semantics=sem, vmem_limit_bytes=vmem)


def _seq_bounds(row, groups):
    (b1, s1), (_, s2) = groups
    t1 = b1 * s1
    in1 = row < t1
    start = jnp.where(in1, (row // s1) * s1, t1 + ((row - t1) // s2) * s2)
    return start, start + jnp.where(in1, s1, s2)


def _split3(a):
    hi = a.astype(BF16)
    r1 = a - hi.astype(F32)
    mid = r1.astype(BF16)
    lo = (r1 - mid.astype(F32)).astype(BF16)
    return hi, mid, lo


def _dot_split(a, b, pieces):
    parts = _split3(a)[:pieces]
    out = jnp.dot(parts[0], b, preferred_element_type=F32)
    for p in parts[1:]:
        out = out + jnp.dot(p, b, preferred_element_type=F32)
    return out


def _group_sumsq(z, bd_ref):
    return _dot_split(z * z, bd_ref[...], 2)


def _inproj_kernel(x_ref, g1_ref, w_ref, wgf_ref, wgb_ref, qg_ref, kg_ref, cos_ref, sinb_ref, sinc_ref,
                   bd_ref, p4_ref, p16_ref,
                   q1_ref, k1_ref, v1_ref, q4_ref, k4_ref, v4_ref, q16_ref, k16_ref, v16_ref,
                   qkm_ref, vm_ref, om_ref, gf_ref, gb_ref):
    x = x_ref[...]
    xn = x * lax.rsqrt(jnp.mean(x * x, axis=-1, keepdims=True) + RMS_EPS) * g1_ref[...]
    xb = xn.astype(BF16)
    aw, mw = ATTN_WIDTH, MLSTM_WIDTH

    def proj(lo, hi):
        return jnp.dot(xb, w_ref[:, lo:hi], preferred_element_type=F32)

    reps = aw // LANES
    cos_a = jnp.concatenate([cos_ref[...]] * reps, axis=1)
    sin_b = jnp.concatenate([sinb_ref[...]] * reps, axis=1)
    sin_c = jnp.concatenate([sinc_ref[...]] * reps, axis=1)

    def norm_rot(z, g_ref):
        ss = _group_sumsq(z, bd_ref)
        y = z * lax.rsqrt(ss * (1.0 / ATTN_HEAD_DIM) + RMS_EPS) * g_ref[...]
        half = ROT_DIM // 2
        y_up = pltpu.roll(y, aw - half, axis=1)
        y_dn = pltpu.roll(y, half, axis=1)
        return (y * cos_a + y_up * sin_b + y_dn * sin_c).astype(BF16)

    def emit(z, nat_ref, r4_ref, r16_ref):
        nat_ref[...] = z
        r4_ref[...] = jnp.dot(p4_ref[...], z, preferred_element_type=F32).astype(BF16)
        r16_ref[...] = jnp.dot(p16_ref[...], z, preferred_element_type=F32).astype(BF16)

    emit(norm_rot(proj(0, aw), qg_ref), q1_ref, q4_ref, q16_ref)
    emit(norm_rot(proj(aw, 2 * aw), kg_ref), k1_ref, k4_ref, k16_ref)
    emit(proj(2 * aw, 3 * aw).astype(BF16), v1_ref, v4_ref, v16_ref)
    qkm_ref[...] = proj(3 * aw, 3 * aw + 2 * mw)
    vm_ref[...] = proj(3 * aw + 2 * mw, 3 * aw + 3 * mw).astype(BF16)
    om_ref[...] = proj(3 * aw + 3 * mw, 3 * aw + 4 * mw)
    gf_ref[...] = jnp.dot(xn, wgf_ref[...], precision=HIGHEST, preferred_element_type=F32)
    gb_ref[...] = jnp.dot(xn, wgb_ref[...], precision=HIGHEST, preferred_element_type=F32)


def _inproj(x2, lw, tables, consts, groups):
    T = x2.shape[0]
    tm = TILE
    aw, mw, ng = ATTN_WIDTH, MLSTM_WIDTH, 2 * MLSTM_HEADS
    row = lambda w: pl.BlockSpec((tm, w), lambda i: (i, 0))
    const = lambda a: pl.BlockSpec(a.shape, lambda i: (0, 0))

    def tab_index(i):
        start, _ = _seq_bounds(i * tm, groups)
        return (i - start // tm, 0)

    tab = pl.BlockSpec((tm, LANES), tab_index)
    weights = (lw['g1'], lw['w_main'], lw['w_gate'][0], lw['w_gate'][1], lw['qg'], lw['kg'])
    perms = (consts['bd'], consts['p4'], consts['p16'])
    bf = lambda w: jax.ShapeDtypeStruct((T, w), BF16)
    f32 = lambda w: jax.ShapeDtypeStruct((T, w), F32)
    return pl.pallas_call(
        _inproj_kernel,
        grid=(T // tm,),
        in_specs=[row(D_MODEL)] + [const(a) for a in weights] + [tab, tab, tab] + [const(a) for a in perms],
        out_specs=[row(aw)] * 9 + [row(2 * mw), row(mw), row(mw), row(ng), row(ng)],
        out_shape=[bf(aw)] * 9 + [f32(2 * mw), bf(mw), f32(mw), f32(ng), f32(ng)],
        compiler_params=_params("parallel"),
        name="inproj",
    )(x2, *weights, *tables, *perms)


def _attn_kernel(q_ref, kl_ref, kc_ref, kr_ref, vl_ref, vc_ref, vr_ref, o_ref, lse_ref, *,
                 dilation, groups):
    aw = ATTN_WIDTH
    lq, lk = LQ, LQ + 2 * HALO
    row0 = pl.program_id(0) * (LQ * dilation)
    start, end = _seq_bounds(row0, groups)
    m0 = (row0 - start) // dilation
    sub_len = (end - start) // dilation

    flat = lambda ref, n: ref[...].reshape(n, ref.shape[-1])
    q = flat(q_ref, lq)
    k = jnp.concatenate([flat(kl_ref, HALO), flat(kc_ref, lq), flat(kr_ref, HALO)], axis=0)
    v = jnp.concatenate([flat(vl_ref, HALO), flat(vc_ref, lq), flat(vr_ref, HALO)], axis=0)

    row = lax.broadcasted_iota(jnp.int32, (2 * lq, lk), 0)
    col = lax.broadcasted_iota(jnp.int32, (2 * lq, lk), 1)
    qrow = jnp.where(row >= lq, row - lq, row)
    rel = col - HALO - qrow
    kpos = m0 - HALO + col
    mask = (jnp.abs(rel) <= HALO) & (kpos >= 0) & (kpos < sub_len)

    lane2 = lax.broadcasted_iota(jnp.int32, (2 * lq, LANES), 1)
    row2 = lax.broadcasted_iota(jnp.int32, (2 * lq, LANES), 0)
    own = (lane2 >= ATTN_HEAD_DIM) ^ (row2 < lq)
    lane = lax.broadcasted_iota(jnp.int32, (lq, LANES), 1)
    first = lane < ATTN_HEAD_DIM

    lse_tile = jnp.zeros((lq, LANES), F32)
    outs = []
    for hp in range(aw // LANES):
        sl = slice(hp * LANES, (hp + 1) * LANES)
        qp = q[:, sl]
        q2 = jnp.concatenate([qp, qp], axis=0)
        q2 = jnp.where(own, q2, jnp.zeros_like(q2))
        s = lax.dot_general(q2, k[:, sl], NT_DIMS, preferred_element_type=F32) * (ATTN_HEAD_DIM ** -0.5)
        s = jnp.where(mask, s, NEG_INF)
        mx = jnp.max(s, axis=-1, keepdims=True)
        p = jnp.exp(s - mx)
        den = jnp.sum(p, axis=-1, keepdims=True)
        pv = jnp.dot(p.astype(BF16), v[:, sl], preferred_element_type=F32)
        o2 = pv / den
        lse2 = mx + jnp.log(den)
        outs.append(jnp.where(first, o2[:lq], o2[lq:]).astype(BF16))
        lse_tile = jnp.where(lane == 2 * hp, lse2[:lq], lse_tile)
        lse_tile = jnp.where(lane == 2 * hp + 1, lse2[lq:], lse_tile)
    o_ref[...] = jnp.concatenate(outs, axis=1).reshape(o_ref.shape)
    lse_ref[...] = lse_tile.reshape(lse_ref.shape)


def _attn_segment(q, k, v, dilation, groups):
    T, aw = q.shape
    tm = TILE
    per_tile = tm // dilation
    ntile = T // tm
    if per_tile >= LQ:
        nblk = per_tile // LQ
        nhalo = per_tile // HALO
        hb = LQ // HALO
        shape = lambda w: (ntile * dilation, per_tile, w)
        grid = (ntile * nblk, dilation)
        slab = lambda i, r: (i // nblk) * dilation + r

        def cen(w):
            return pl.BlockSpec((1, LQ, w), lambda i, r: (slab(i, r), i % nblk, 0))

        def halo_spec(side):
            def index(i, r):
                j = (i % nblk) * hb + (-1 if side < 0 else hb)
                t = i // nblk + jnp.where(j < 0, -1, 0) + jnp.where(j >= nhalo, 1, 0)
                t = jnp.clip(t, 0, ntile - 1)
                return (t * dilation + r, j % nhalo, 0)
            return pl.BlockSpec((1, HALO, aw), index)
    else:
        tq = LQ // per_tile
        th = HALO // per_tile
        nhb = ntile // th
        shape = lambda w: (ntile, dilation, per_tile, w)
        grid = (ntile // tq, dilation)

        def cen(w):
            return pl.BlockSpec((tq, None, per_tile, w), lambda i, r: (i, r, 0, 0))

        def halo_spec(side):
            def index(i, r):
                j = i * (tq // th) + (-1 if side < 0 else tq // th)
                return (jnp.clip(j, 0, nhb - 1), r, 0, 0)
            return pl.BlockSpec((th, None, per_tile, aw), index)

    left, right = halo_spec(-1), halo_spec(1)
    view = lambda a: a.reshape(shape(a.shape[-1]))
    o, lse = pl.pallas_call(
        functools.partial(_attn_kernel, dilation=dilation, groups=groups),
        grid=grid,
        in_specs=[cen(aw), left, cen(aw), right, left, cen(aw), right],
        out_specs=[cen(aw), cen(LANES)],
        out_shape=[jax.ShapeDtypeStruct(shape(aw), BF16), jax.ShapeDtypeStruct(shape(LANES), F32)],
        compiler_params=_params("parallel", "parallel"),
        name=f"attn_d{dilation}",
    )(view(q), view(k), view(k), view(k), view(v), view(v), view(v))
    return o.reshape(T, aw), lse.reshape(T, LANES)


def _log_sigmoid(x):
    return jnp.minimum(x, 0.0) - jnp.log(1.0 + jnp.exp(-jnp.abs(x)))


def _mlstm_kernel(*refs, tb, reverse, first_pass, groups):
    nh, hd = MLSTM_HEADS, MLSTM_HEAD_DIM
    mw = MLSTM_WIDTH
    if first_pass:
        (qk_ref, qkp_ref, qkn_ref, cw_ref, cb_ref, v_ref, gc_ref, gr_ref, bc_ref, br_ref,
         h_ref, qo_ref, ko_ref, c_scr, m_scr, q_scr, k_scr, xe_scr) = refs
    else:
        (q_ref, k_ref, v_ref, gc_ref, gr_ref, bc_ref, br_ref, hf_ref, om_ref, og_ref,
         h_ref, c_scr, m_scr) = refs
    step = pl.program_id(0)
    blk = pl.num_programs(0) - 1 - step if reverse else step
    row0 = blk * tb
    start, end = _seq_bounds(row0, groups)
    at_start = row0 == start
    at_end = row0 + tb == end

    @pl.when(at_end if reverse else at_start)
    def _():
        c_scr[...] = jnp.zeros_like(c_scr)
        m_scr[...] = jnp.zeros_like(m_scr)

    if first_pass:
        pad = CONV_WIDTH // 2
        xe_scr[0:SUBLANES, :] = jnp.where(at_start, 0.0, qkp_ref[...])
        xe_scr[SUBLANES:SUBLANES + tb, :] = qk_ref[...]
        xe_scr[SUBLANES + tb:, :] = jnp.where(at_end, 0.0, qkn_ref[...])
        acc = jnp.zeros((tb, 2 * mw), F32) + cb_ref[...]
        for j in range(CONV_WIDTH):
            acc = acc + xe_scr[SUBLANES - pad + j:SUBLANES - pad + j + tb, :] * cw_ref[j:j + 1, :]
        act = acc * jax.nn.sigmoid(acc)
        qb = act[:, :mw].astype(BF16)
        kb = (act[:, mw:] * (hd ** -0.5)).astype(BF16)
        qo_ref[...] = qb
        ko_ref[...] = kb
        q_scr[...] = qb
        k_scr[...] = kb
        q_src, k_src = q_scr, k_scr
    else:
        q_src, k_src = q_ref, k_ref

    ti = lax.broadcasted_iota(jnp.int32, (CHUNK, CHUNK), 0)
    si = lax.broadcasted_iota(jnp.int32, (CHUNK, CHUNK), 1)
    causal = (si >= ti) if reverse else (si <= ti)
    cum_col = causal.astype(F32)
    cum_row = ((ti >= si) if reverse else (ti <= si)).astype(F32)
    last = 0 if reverse else CHUNK - 1
    ones_col = (lax.broadcasted_iota(jnp.int32, (CHUNK, hd), 1) == 0).astype(BF16)
    ngate = 2 * nh
    gc_is_i = lax.broadcasted_iota(jnp.int32, (CHUNK, ngate), 1) < nh
    gr_is_i = lax.broadcasted_iota(jnp.int32, (ngate, CHUNK), 0) < nh
    nchunk = tb // CHUNK

    def chunk_body(j, carry):
        jj = nchunk - 1 - j if reverse else j
        r0 = pl.multiple_of(jj * CHUNK, CHUNK)
        rows = pl.ds(r0, CHUNK)
        a_c = gc_ref[rows, :] + bc_ref[...]
        a_r = gr_ref[jj] + br_ref[...]
        gl_c = jnp.where(gc_is_i, a_c, _log_sigmoid(a_c))
        gl_r = jnp.where(gr_is_i, a_r, _log_sigmoid(a_r))
        b_c = jnp.dot(cum_col, gl_c, precision=HIGHEST, preferred_element_type=F32)
        b_r = jnp.dot(gl_r, cum_row, precision=HIGHEST, preferred_element_type=F32)
        qc = q_src[rows, :]
        kc = k_src[rows, :]
        vc = v_ref[rows, :]
        if not first_pass:
            hfc = hf_ref[rows, :]
            omc = om_ref[rows, :]
        for h in range(nh):
            hs = slice(h * hd, (h + 1) * hd)
            bc = b_c[:, nh + h:nh + h + 1]
            br = b_r[nh + h:nh + h + 1, :]
            lir = gl_r[h:h + 1, :]
            lic = gl_c[:, h:h + 1]
            m_prev = m_scr[h:h + 1, 0:1]
            dmat = jnp.where(causal, bc - br + lir, NEG_INF)
            inter = bc + m_prev
            m_t = jnp.maximum(inter, jnp.max(dmat, axis=-1, keepdims=True))
            w_intra = jnp.exp(dmat - m_t)
            w_inter = jnp.exp(inter - m_t)
            qh, kh, vh = qc[:, hs], kc[:, hs], vc[:, hs]
            s = lax.dot_general(qh, kh, NT_DIMS, preferred_element_type=F32) * w_intra
            v_aug = jnp.concatenate([vh, ones_col], axis=1)
            c_old = c_scr[h]
            h_aug = (jnp.dot(s.astype(BF16), v_aug, preferred_element_type=F32)
                     + w_inter * jnp.dot(qh, c_old.astype(BF16), preferred_element_type=F32))
            num = h_aug[:, :hd]
            den = h_aug[:, hd:hd + 1]
            h_out = num / jnp.maximum(jnp.abs(den), jnp.exp(-m_t))
            b_last = bc[last:last + 1, :]
            log_w = b_last - bc + lic
            m_new = jnp.maximum(b_last + m_prev, jnp.max(log_w, axis=0, keepdims=True))
            ws = jnp.exp(log_w - m_new)
            keep = jnp.exp(b_last + m_prev - m_new)
            wv = (ws * v_aug.astype(F32)).astype(BF16)
            c_scr[h] = keep * c_old + lax.dot_general(kh, wv, TN_DIMS, preferred_element_type=F32)
            m_scr[h:h + 1, :] = jnp.broadcast_to(m_new, (1, LANES))
            if first_pass:
                h_ref[rows, hs] = h_out
            else:
                y = jax.nn.sigmoid(omc[:, hs]) * (hfc[:, hs] + h_out)
                y = y * lax.rsqrt(jnp.mean(y * y, axis=-1, keepdims=True) + RMS_EPS) * og_ref[:, hs]
                h_ref[rows, hs] = y
        return carry

    lax.fori_loop(0, nchunk, chunk_body, 0)


def _mlstm_pass(reverse, first_pass, operands, groups):
    mw, nh, hd = MLSTM_WIDTH, MLSTM_HEADS, MLSTM_HEAD_DIM
    tb = TILE
    T = operands[0].shape[0]
    nblk = T // tb
    ngate = 2 * nh
    bidx = (lambda s: nblk - 1 - s) if reverse else (lambda s: s)
    rows = lambda w: pl.BlockSpec((tb, w), lambda s: (bidx(s), 0))
    const = lambda a: pl.BlockSpec(a.shape, lambda s: (0,) * a.ndim)
    gate_rows = pl.BlockSpec((tb // CHUNK, ngate, CHUNK), lambda s: (bidx(s), 0, 0))
    hb = tb // SUBLANES
    nhb = T // SUBLANES
    state = [pltpu.VMEM((nh, hd, 2 * hd), F32), pltpu.VMEM((SUBLANES, LANES), F32)]
    if first_pass:
        qk, cw, cb, v, gc, gr, bc, br = operands
        prev = pl.BlockSpec((SUBLANES, 2 * mw), lambda s: (jnp.maximum(bidx(s) * hb - 1, 0), 0))
        nxt = pl.BlockSpec((SUBLANES, 2 * mw), lambda s: (jnp.minimum((bidx(s) + 1) * hb, nhb - 1), 0))
        in_specs = [rows(2 * mw), prev, nxt, const(cw), const(cb), rows(mw), rows(ngate), gate_rows,
                    const(bc), const(br)]
        args = (qk, qk, qk, cw, cb, v, gc, gr, bc, br)
        out_specs = [rows(mw), rows(mw), rows(mw)]
        out_shape = [jax.ShapeDtypeStruct((T, mw), F32), jax.ShapeDtypeStruct((T, mw), BF16),
                     jax.ShapeDtypeStruct((T, mw), BF16)]
        scratch = state + [pltpu.VMEM((tb, mw), BF16), pltpu.VMEM((tb, mw), BF16),
                           pltpu.VMEM((tb + 2 * SUBLANES, 2 * mw), F32)]
    else:
        q, k, v, gc, gr, bc, br, hf, om, og = operands
        in_specs = [rows(mw), rows(mw), rows(mw), rows(ngate), gate_rows, const(bc), const(br),
                    rows(mw), rows(mw), const(og)]
        args = operands
        out_specs = rows(mw)
        out_shape = jax.ShapeDtypeStruct((T, mw), F32)
        scratch = state
    return pl.pallas_call(
        functools.partial(_mlstm_kernel, tb=tb, reverse=reverse, first_pass=first_pass, groups=groups),
        grid=(nblk,),
        in_specs=in_specs,
        out_specs=out_specs,
        out_shape=out_shape,
        scratch_shapes=scratch,
        compiler_params=_params("arbitrary"),
        name="mlstm_fwd" if first_pass else "mlstm_bwd",
    )(*args)


def _outproj_kernel(o1_ref, o4_ref, o16_ref, l1_ref, l4_ref, l16_ref, ml_ref, x_ref, ag_ref, bd_ref,
                    p4t_ref, p16t_ref, ex_ref, w_ref, y_ref):
    unperm = lambda p_ref, o_ref: jnp.dot(p_ref[...], o_ref[...], preferred_element_type=F32)
    o1 = o1_ref[...].astype(F32)
    o4 = unperm(p4t_ref, o4_ref)
    o16 = unperm(p16t_ref, o16_ref)

    def unperm_f32(p_ref, l_ref):
        parts = _split3(l_ref[...])
        return sum(jnp.dot(p_ref[...], part, preferred_element_type=F32) for part in parts)

    l1 = l1_ref[...]
    l4 = unperm_f32(p4t_ref, l4_ref)
    l16 = unperm_f32(p16t_ref, l16_ref)
    mx = jnp.maximum(jnp.maximum(l1, l4), l16)
    e1, e4, e16 = jnp.exp(l1 - mx), jnp.exp(l4 - mx), jnp.exp(l16 - mx)
    inv = 1.0 / (e1 + e4 + e16)
    expand = lambda w: _dot_split(w, ex_ref[...], 2)
    attn = expand(e1 * inv) * o1 + expand(e4 * inv) * o4 + expand(e16 * inv) * o16
    ss = _group_sumsq(attn, bd_ref)
    attn = attn * lax.rsqrt(ss * (1.0 / ATTN_HEAD_DIM) + RMS_EPS) * ag_ref[...]
    aw = ATTN_WIDTH
    y = jnp.dot(attn.astype(BF16), w_ref[:aw, :], preferred_element_type=F32)
    y = y + jnp.dot(ml_ref[...].astype(BF16), w_ref[aw:, :], preferred_element_type=F32)
    y_ref[...] = x_ref[...] + y


def _outproj(os_, ls_, ml, x2, lw, consts):
    T = x2.shape[0]
    tm = TILE
    row = lambda w: pl.BlockSpec((tm, w), lambda i: (i, 0))
    const = lambda a: pl.BlockSpec(a.shape, lambda i: (0, 0))
    aw = ATTN_WIDTH
    cs = (lw['ag'], consts['bd'], consts['p4t'], consts['p16t'], consts['expand'], lw['w_out'])
    return pl.pallas_call(
        _outproj_kernel,
        grid=(T // tm,),
        in_specs=[row(aw)] * 3 + [row(LANES)] * 3 + [row(MLSTM_WIDTH), row(D_MODEL)] + [const(a) for a in cs],
        out_specs=row(D_MODEL),
        out_shape=jax.ShapeDtypeStruct((T, D_MODEL), F32),
        compiler_params=_params("parallel"),
        name="outproj",
    )(*os_, *ls_, ml, x2, *cs)


def _router_kernel(x_ref, g_ref, w_ref, b_ref, h_ref, lg_ref):
    x = x_ref[...]
    h = x * lax.rsqrt(jnp.mean(x * x, axis=-1, keepdims=True) + RMS_EPS) * g_ref[...]
    h_ref[...] = h.astype(BF16)
    lg_ref[...] = jnp.dot(h, w_ref[...], precision=HIGHEST, preferred_element_type=F32) + b_ref[...]


def _router(x2, g2, rw, rb):
    T = x2.shape[0]
    tm = TILE
    row = lambda w: pl.BlockSpec((tm, w), lambda i: (i, 0))
    const = lambda a: pl.BlockSpec(a.shape, lambda i: (0, 0))
    return pl.pallas_call(
        _router_kernel,
        grid=(T // tm,),
        in_specs=[row(D_MODEL), const(g2), const(rw), const(rb)],
        out_specs=[row(D_MODEL), row(N_EXPERTS)],
        out_shape=[jax.ShapeDtypeStruct((T, D_MODEL), BF16), jax.ShapeDtypeStruct((T, N_EXPERTS), F32)],
        compiler_params=_params("parallel"),
        name="router",
    )(x2, g2, rw, rb)


def _expert_kernel(blk_ref, exp_ref, lo_ref, hi_ref, x_ref, wu_ref, bu_ref, wd_ref, bd_ref, y_ref,
                   wu_scr, wd_scr, *, bm):
    v = pl.program_id(0)
    pv = jnp.maximum(v - 1, 0)
    new_expert = (v == 0) | (exp_ref[v] != exp_ref[pv])
    first_visit = (v == 0) | (blk_ref[v] != blk_ref[pv])
    cast_rows = 128

    @pl.when(new_expert)
    def _():
        def cast(i, c):
            r = pl.ds(pl.multiple_of(i * cast_rows, cast_rows), cast_rows)
            wu_scr[r, :] = wu_ref[0, 0, r, :].astype(BF16)
            wd_scr[r, :] = wd_ref[0, 0, r, :].astype(BF16)
            return c
        lax.fori_loop(0, D_MODEL // cast_rows, cast, 0)

    lo, hi = lo_ref[v], hi_ref[v]

    @pl.when(hi > lo)
    def _():
        x = x_ref[...]
        half = D_FF // 2
        y = jnp.zeros((bm, D_MODEL), F32) + bd_ref[0, 0]
        for c in range(2):
            g0, l0 = c * half, D_FF + c * half
            x_glu = jnp.dot(x, wu_scr[:, g0:g0 + half], preferred_element_type=F32) + bu_ref[0, 0, :, g0:g0 + half]
            x_lin = jnp.dot(x, wu_scr[:, l0:l0 + half], preferred_element_type=F32) + bu_ref[0, 0, :, l0:l0 + half]
            x_glu = jnp.minimum(x_glu, SWIGLU_LIMIT)
            x_lin = jnp.clip(x_lin, -SWIGLU_LIMIT, SWIGLU_LIMIT)
            act = x_glu * jax.nn.sigmoid(SWIGLU_ALPHA * x_glu) * (x_lin + 1.0)
            y = y + jnp.dot(act.astype(BF16), wd_scr[g0:g0 + half, :], preferred_element_type=F32)
        rows = blk_ref[v] * bm + lax.broadcasted_iota(jnp.int32, (bm, 1), 0)
        mine = (rows >= lo) & (rows < hi)

        @pl.when(first_visit)
        def _():
            y_ref[...] = jnp.where(mine, y, 0.0)

        @pl.when(jnp.logical_not(first_visit))
        def _():
            y_ref[...] = jnp.where(mine, y, y_ref[...])


def _experts(meta, x_sorted, w_up, b_up, w_down, b_down, layer):
    A = x_sorted.shape[0]
    bm = EXPERT_ROWS
    nvisit = meta[0].shape[0]
    grid_spec = pltpu.PrefetchScalarGridSpec(
        num_scalar_prefetch=4,
        grid=(nvisit,),
        in_specs=[pl.BlockSpec((bm, D_MODEL), lambda v, blk, ex, lo, hi: (blk[v], 0)),
                  pl.BlockSpec((1, 1, D_MODEL, 2 * D_FF), lambda v, blk, ex, lo, hi: (layer, ex[v], 0, 0)),
                  pl.BlockSpec((1, 1, 1, 2 * D_FF), lambda v, blk, ex, lo, hi: (layer, ex[v], 0, 0)),
                  pl.BlockSpec((1, 1, D_FF, D_MODEL), lambda v, blk, ex, lo, hi: (layer, ex[v], 0, 0)),
                  pl.BlockSpec((1, 1, 1, D_MODEL), lambda v, blk, ex, lo, hi: (layer, ex[v], 0, 0))],
        out_specs=pl.BlockSpec((bm, D_MODEL), lambda v, blk, ex, lo, hi: (blk[v], 0)),
        scratch_shapes=[pltpu.VMEM((D_MODEL, 2 * D_FF), BF16), pltpu.VMEM((D_FF, D_MODEL), BF16)],
    )
    return pl.pallas_call(
        functools.partial(_expert_kernel, bm=bm),
        grid_spec=grid_spec,
        out_shape=jax.ShapeDtypeStruct((A, D_MODEL), F32),
        compiler_params=_params("arbitrary", vmem=VMEM_LIMIT_EXPERTS),
        name="experts",
    )(*meta, x_sorted, w_up, b_up[:, :, None, :], w_down, b_down[:, :, None, :])


def _combine_kernel(x_ref, y0_ref, y1_ref, y2_ref, y3_ref, g_ref, o_ref):
    acc = x_ref[...]
    g = g_ref[...]
    for k, y_ref in enumerate((y0_ref, y1_ref, y2_ref, y3_ref)):
        acc = acc + y_ref[...] * g[:, k:k + 1]
    o_ref[...] = acc


def _combine(x2, ys, gates):
    T = x2.shape[0]
    tm = TILE
    row = lambda w: pl.BlockSpec((tm, w), lambda i: (i, 0))
    return pl.pallas_call(
        _combine_kernel,
        grid=(T // tm,),
        in_specs=[row(D_MODEL)] * 5 + [row(TOP_K)],
        out_specs=row(D_MODEL),
        out_shape=jax.ShapeDtypeStruct((T, D_MODEL), F32),
        compiler_params=_params("parallel"),
        name="combine",
    )(x2, *ys, gates)


def _visit_schedule(counts, nblk, bm):
    ends = jnp.cumsum(counts)
    starts = ends - counts
    first_blk = starts // bm
    nvis = jnp.where(counts > 0, (ends - 1) // bm - first_blk + 1, 0)
    vis_end = jnp.cumsum(nvis)
    vis_start = vis_end - nvis
    total = vis_end[-1]
    v = jnp.arange(nblk + N_EXPERTS - 1, dtype=jnp.int32)
    e = jnp.clip(jnp.searchsorted(vis_end, v, side='right'), 0, N_EXPERTS - 1).astype(jnp.int32)
    valid = v < total
    e_last = jnp.max(jnp.where(counts > 0, jnp.arange(N_EXPERTS, dtype=jnp.int32), 0))
    e = jnp.where(valid, e, e_last)
    blk = jnp.where(valid, first_blk[e] + v - vis_start[e], nblk - 1)
    lo = jnp.where(valid, starts[e], 0)
    hi = jnp.where(valid, ends[e], 0)
    i32 = lambda a: a.astype(jnp.int32)
    return i32(blk), i32(e), i32(lo), i32(hi)


def _moe(x2, lw, w_up, b_up, w_down, b_down, layer):
    T = x2.shape[0]
    A = T * TOP_K
    bm = EXPERT_ROWS
    assert A % bm == 0
    h, logits = _router(x2, lw['g2'], lw['rw'], lw['rb'])
    top_val, top_idx = lax.top_k(logits, TOP_K)
    gates = jax.nn.softmax(top_val, axis=-1)
    flat_e = top_idx.reshape(-1).astype(jnp.int32)
    iota = jnp.arange(A, dtype=jnp.int32)
    _, order = lax.sort((flat_e, iota), num_keys=1, is_stable=True)
    _, inv = lax.sort((order, iota), num_keys=1)
    counts = jnp.sum((flat_e[:, None] == jnp.arange(N_EXPERTS, dtype=jnp.int32)[None, :]).astype(jnp.int32),
                     axis=0)
    x_sorted = jnp.take(h, order // TOP_K, axis=0, mode='clip')
    meta = _visit_schedule(counts, A // bm, bm)
    y_sorted = _experts(meta, x_sorted, w_up, b_up, w_down, b_down, layer)
    inv4 = inv.reshape(T, TOP_K)
    ys = [jnp.take(y_sorted, inv4[:, k], axis=0, mode='clip') for k in range(TOP_K)]
    return _combine(x2, ys, gates)


def _rotary_tables(seq):
    half = ROT_DIM // 2
    inv_freq = ROPE_THETA ** (-jnp.arange(0, ROT_DIM, 2, dtype=F32) / ROT_DIM)
    ang = jnp.arange(seq).astype(F32)[:, None] * inv_freq[None, :]
    cos, sin = jnp.cos(ang), jnp.sin(ang)
    rest = ATTN_HEAD_DIM - ROT_DIM
    ones = jnp.ones((seq, rest), F32)
    zeros = jnp.zeros((seq, rest), F32)
    zh = jnp.zeros((seq, half), F32)
    cos_a = jnp.concatenate([cos, cos, ones], axis=1)
    sin_b = jnp.concatenate([-sin, zh, zeros], axis=1)
    sin_c = jnp.concatenate([zh, sin, zeros], axis=1)
    tile = lambda a: jnp.tile(a, (1, LANES // ATTN_HEAD_DIM))
    return tile(cos_a), tile(sin_b), tile(sin_c)


def _constants():
    idx = jnp.arange(ATTN_WIDTH) // ATTN_HEAD_DIM
    bd = (idx[:, None] == idx[None, :]).astype(BF16)

    def perm(d):
        new = jnp.arange(TILE)
        old = (new % (TILE // d)) * d + new // (TILE // d)
        return (old[:, None] == jnp.arange(TILE)[None, :]).astype(BF16)

    p4, p16 = perm(4), perm(16)
    expand = (jnp.arange(LANES)[:, None] == idx[None, :]).astype(BF16)
    return dict(bd=bd, p4=p4, p16=p16, p4t=p4.T, p16t=p16.T, expand=expand)


def _layer_weights(l, norm1_g, w_in, q_norm_g, k_norm_g, attn_out_g, conv_w, conv_b, igate_b, fgate_b,
                   mlstm_out_g, w_out, norm2_g, router_w, router_b):
    aw, mw, nh = ATTN_WIDTH, MLSTM_WIDTH, MLSTM_HEADS
    n_main = 3 * aw + 4 * mw
    w = w_in[l]
    wg = w[:, n_main:]
    pick = lambda a, d: jnp.concatenate([a[..., d * nh:(d + 1) * nh],
                                         a[..., (2 + d) * nh:(3 + d) * nh]], axis=-1)
    gate_bias = [jnp.concatenate([igate_b[l, d], fgate_b[l, d]]) for d in range(2)]
    return dict(
        g1=norm1_g[l][None, :], w_main=w[:, :n_main].astype(BF16),
        w_gate=[pick(wg, 0), pick(wg, 1)],
        qg=jnp.tile(q_norm_g[l], ATTN_HEADS)[None, :], kg=jnp.tile(k_norm_g[l], ATTN_HEADS)[None, :],
        ag=attn_out_g[l][None, :], cw=conv_w[l], cb=conv_b[l][None, :],
        bias_c=[b[None, :] for b in gate_bias], bias_r=[b[:, None] for b in gate_bias],
        og=mlstm_out_g[l][None, :], w_out=w_out[l].astype(BF16), g2=norm2_g[l][None, :],
        rw=router_w[l], rb=router_b[l][None, :])


def kernel(x_prompt, x_sample, norm1_g, w_in, q_norm_g, k_norm_g, attn_out_g, conv_w, conv_b, igate_b,
           fgate_b, mlstm_out_g, w_out, norm2_g, router_w, router_b, w_up, b_up, w_down, b_down):
    assert all(w // (2 * d) == HALO for w, d in SEGMENTS)
    groups = (x_prompt.shape[:2], x_sample.shape[:2])
    span = max(d for _, d in SEGMENTS) * LQ
    assert all(s % span == 0 for _, s in groups), groups
    small = (norm1_g, w_in, q_norm_g, k_norm_g, attn_out_g, conv_w, conv_b, igate_b, fgate_b,
             mlstm_out_g, w_out, norm2_g, router_w, router_b)
    consts = _constants()
    tables = _rotary_tables(max(s for _, s in groups))
    x2 = jnp.concatenate([x_prompt.reshape(-1, D_MODEL), x_sample.reshape(-1, D_MODEL)], axis=0)
    T = x2.shape[0]
    ngate = 2 * MLSTM_HEADS
    chunk_rows = lambda g: jnp.swapaxes(g.reshape(T // CHUNK, CHUNK, ngate), 1, 2)
    for l in range(DEPTH):
        lw = _layer_weights(l, *small)
        (q1, k1, v1, q4, k4, v4, q16, k16, v16, qkm, vm, om, gf, gb) = _inproj(x2, lw, tables, consts, groups)
        segs = [_attn_segment(q1, k1, v1, 1, groups), _attn_segment(q4, k4, v4, 4, groups),
                _attn_segment(q16, k16, v16, 16, groups)]
        hf, qm, km = _mlstm_pass(False, True, (qkm, lw['cw'], lw['cb'], vm, gf, chunk_rows(gf),
                                               lw['bias_c'][0], lw['bias_r'][0]), groups)
        ml = _mlstm_pass(True, False, (qm, km, vm, gb, chunk_rows(gb), lw['bias_c'][1], lw['bias_r'][1],
                                       hf, om, lw['og']), groups)
        x2 = _outproj([s[0] for s in segs], [s[1] for s in segs], ml, x2, lw, consts)
        x2 = _moe(x2, lw, w_up, b_up, w_down, b_down, l)
    t1 = x_prompt.shape[0] * x_prompt.shape[1]
    return (x2[:t1].reshape(x_prompt.shape), x2[t1:].reshape(x_sample.shape))
```

```python
import functools

import jax
import jax.numpy as jnp
from jax import lax
from jax.experimental import pallas as pl
from jax.experimental.pallas import tpu as pltpu

D_MODEL = 1024
DEPTH = 2
ATTN_WIDTH = 512
ATTN_HEAD_DIM = 64
ATTN_HEADS = 8
ROT_DIM = 16
ROPE_THETA = 500000.0
SEGMENTS = ((128, 1), (512, 4), (2048, 16))
HALO = 64
MLSTM_WIDTH = 512
MLSTM_HEAD_DIM = 128
MLSTM_HEADS = 4
CONV_WIDTH = 5
CHUNK = 64
N_EXPERTS = 32
TOP_K = 4
D_FF = 1024
SWIGLU_LIMIT = 7.0
SWIGLU_ALPHA = 1.702
RMS_EPS = 1e-6
NEG_INF = -1e30

LANES = 128
SUBLANES = 8
TILE = 512
LQ = 128
EXPERT_ROWS = 512
VMEM_LIMIT = 48 * 1024 * 1024
VMEM_LIMIT_EXPERTS = 56 * 1024 * 1024

F32 = jnp.float32
BF16 = jnp.bfloat16
NT_DIMS = (((1,), (1,)), ((), ()))
TN_DIMS = (((0,), (0,)), ((), ()))


def _params(*sem, vmem=VMEM_LIMIT):
    return pltpu.CompilerParams(dimension_semantics=sem, vmem_limit_bytes=vmem)


def _seq_bounds(row, groups):
    (b1, s1), (_, s2) = groups
    t1 = b1 * s1
    in1 = row < t1
    start = jnp.where(in1, (row // s1) * s1, t1 + ((row - t1) // s2) * s2)
    return start, start + jnp.where(in1, s1, s2)


def _split3(a):
    hi = a.astype(BF16)
    r1 = a - hi.astype(F32)
    mid = r1.astype(BF16)
    lo = (r1 - mid.astype(F32)).astype(BF16)
    return hi, mid, lo


def _dot_split(a, b, pieces):
    parts = _split3(a)[:pieces]
    out = jnp.dot(parts[0], b, preferred_element_type=F32)
    for p in parts[1:]:
        out = out + jnp.dot(p, b, preferred_element_type=F32)
    return out


def _dot_split_rhs(a, b, pieces):
    parts = _split3(b)[:pieces]
    out = jnp.dot(a, parts[0], preferred_element_type=F32)
    for p in parts[1:]:
        out = out + jnp.dot(a, p, preferred_element_type=F32)
    return out


def _dot_f32(a, b):
    ah, al, _ = _split3(a)
    bh, bl, _ = _split3(b)
    out = jnp.dot(ah, bh, preferred_element_type=F32)
    out = out + jnp.dot(ah, bl, preferred_element_type=F32)
    return out + jnp.dot(al, bh, preferred_element_type=F32)


def _group_sumsq(z, bd_ref):
    return _dot_split(z * z, bd_ref[...], 2)


def _inproj_kernel(x_ref, g1_ref, w_ref, wg_ref, qg_ref, kg_ref, cos_ref, sinb_ref, sinc_ref,
                   bd_ref, p4_ref, p16_ref,
                   q1_ref, k1_ref, v1_ref, q4_ref, k4_ref, v4_ref, q16_ref, k16_ref, v16_ref,
                   qkm_ref, vm_ref, om_ref, gf_ref, gb_ref):
    x = x_ref[...]
    xn = x * lax.rsqrt(jnp.mean(x * x, axis=-1, keepdims=True) + RMS_EPS) * g1_ref[...]
    xb = xn.astype(BF16)
    aw, mw = ATTN_WIDTH, MLSTM_WIDTH

    def proj(lo, hi):
        return jnp.dot(xb, w_ref[:, lo:hi], preferred_element_type=F32)

    reps = aw // LANES
    cos_a = jnp.concatenate([cos_ref[...]] * reps, axis=1)
    sin_b = jnp.concatenate([sinb_ref[...]] * reps, axis=1)
    sin_c = jnp.concatenate([sinc_ref[...]] * reps, axis=1)

    def norm_rot(z, g_ref):
        ss = _group_sumsq(z, bd_ref)
        y = z * lax.rsqrt(ss * (1.0 / ATTN_HEAD_DIM) + RMS_EPS) * g_ref[...]
        half = ROT_DIM // 2
        y_up = pltpu.roll(y, aw - half, axis=1)
        y_dn = pltpu.roll(y, half, axis=1)
        return (y * cos_a + y_up * sin_b + y_dn * sin_c).astype(BF16)

    def emit(z, nat_ref, r4_ref, r16_ref):
        nat_ref[...] = z
        r4_ref[...] = jnp.dot(p4_ref[...], z, preferred_element_type=F32).astype(BF16)
        r16_ref[...] = jnp.dot(p16_ref[...], z, preferred_element_type=F32).astype(BF16)

    emit(norm_rot(proj(0, aw), qg_ref), q1_ref, q4_ref, q16_ref)
    emit(norm_rot(proj(aw, 2 * aw), kg_ref), k1_ref, k4_ref, k16_ref)
    emit(proj(2 * aw, 3 * aw).astype(BF16), v1_ref, v4_ref, v16_ref)
    qkm_ref[...] = proj(3 * aw, 3 * aw + 2 * mw)
    vm_ref[...] = proj(3 * aw + 2 * mw, 3 * aw + 3 * mw).astype(BF16)
    om_ref[...] = proj(3 * aw + 3 * mw, 3 * aw + 4 * mw)
    gates = _dot_f32(xn, wg_ref[...])
    ng = 2 * MLSTM_HEADS
    gf_ref[...] = gates[:, :ng]
    gb_ref[...] = gates[:, ng:]


def _inproj(x2, lw, tables, consts, groups):
    T = x2.shape[0]
    tm = TILE
    aw, mw, ng = ATTN_WIDTH, MLSTM_WIDTH, 2 * MLSTM_HEADS
    row = lambda w: pl.BlockSpec((tm, w), lambda i: (i, 0))
    const = lambda a: pl.BlockSpec(a.shape, lambda i: (0, 0))

    def tab_index(i):
        start, _ = _seq_bounds(i * tm, groups)
        return (i - start // tm, 0)

    tab = pl.BlockSpec((tm, LANES), tab_index)
    weights = (lw['g1'], lw['w_main'], lw['w_gate'], lw['qg'], lw['kg'])
    perms = (consts['bd'], consts['p4'], consts['p16'])
    bf = lambda w: jax.ShapeDtypeStruct((T, w), BF16)
    f32 = lambda w: jax.ShapeDtypeStruct((T, w), F32)
    return pl.pallas_call(
        _inproj_kernel,
        grid=(T // tm,),
        in_specs=[row(D_MODEL)] + [const(a) for a in weights] + [tab, tab, tab] + [const(a) for a in perms],
        out_specs=[row(aw)] * 9 + [row(2 * mw), row(mw), row(mw), row(ng), row(ng)],
        out_shape=[bf(aw)] * 9 + [f32(2 * mw), bf(mw), f32(mw), f32(ng), f32(ng)],
        compiler_params=_params("parallel"),
        name="inproj",
    )(x2, *weights, *tables, *perms)


def _attn_kernel(q_ref, kl_ref, kc_ref, kr_ref, vl_ref, vc_ref, vr_ref, o_ref, lse_ref, *,
                 dilation, groups):
    aw = ATTN_WIDTH
    lq, lk = LQ, LQ + 2 * HALO
    row0 = pl.program_id(0) * (LQ * dilation)
    start, end = _seq_bounds(row0, groups)
    m0 = (row0 - start) // dilation
    sub_len = (end - start) // dilation

    flat = lambda ref, n: ref[...].reshape(n, ref.shape[-1])
    q = flat(q_ref, lq)
    k = jnp.concatenate([flat(kl_ref, HALO), flat(kc_ref, lq), flat(kr_ref, HALO)], axis=0)
    v = jnp.concatenate([flat(vl_ref, HALO), flat(vc_ref, lq), flat(vr_ref, HALO)], axis=0)

    row = lax.broadcasted_iota(jnp.int32, (2 * lq, lk), 0)
    col = lax.broadcasted_iota(jnp.int32, (2 * lq, lk), 1)
    qrow = jnp.where(row >= lq, row - lq, row)
    rel = col - HALO - qrow
    kpos = m0 - HALO + col
    mask = (jnp.abs(rel) <= HALO) & (kpos >= 0) & (kpos < sub_len)

    lane2 = lax.broadcasted_iota(jnp.int32, (2 * lq, LANES), 1)
    row2 = lax.broadcasted_iota(jnp.int32, (2 * lq, LANES), 0)
    own = (lane2 >= ATTN_HEAD_DIM) ^ (row2 < lq)
    lane = lax.broadcasted_iota(jnp.int32, (lq, LANES), 1)
    first = lane < ATTN_HEAD_DIM

    lse_tile = jnp.zeros((lq, LANES), F32)
    outs = []
    for hp in range(aw // LANES):
        sl = slice(hp * LANES, (hp + 1) * LANES)
        qp = q[:, sl]
        q2 = jnp.concatenate([qp, qp], axis=0)
        q2 = jnp.where(own, q2, jnp.zeros_like(q2))
        s = lax.dot_general(q2, k[:, sl], NT_DIMS, preferred_element_type=F32) * (ATTN_HEAD_DIM ** -0.5)
        s = jnp.where(mask, s, NEG_INF)
        mx = jnp.max(s, axis=-1, keepdims=True)
        p = jnp.exp(s - mx)
        den = jnp.sum(p, axis=-1, keepdims=True)
        pv = jnp.dot(p.astype(BF16), v[:, sl], preferred_element_type=F32)
        o2 = pv / den
        lse2 = mx + jnp.log(den)
        outs.append(jnp.where(first, o2[:lq], o2[lq:]).astype(BF16))
        lse_tile = jnp.where(lane == 2 * hp, lse2[:lq], lse_tile)
        lse_tile = jnp.where(lane == 2 * hp + 1, lse2[lq:], lse_tile)
    o_ref[...] = jnp.concatenate(outs, axis=1).reshape(o_ref.shape)
    lse_ref[...] = lse_tile.reshape(lse_ref.shape)


def _attn_segment(q, k, v, dilation, groups):
    T, aw = q.shape
    tm = TILE
    per_tile = tm // dilation
    ntile = T // tm
    if per_tile >= LQ:
        nblk = per_tile // LQ
        nhalo = per_tile // HALO
        hb = LQ // HALO
        shape = lambda w: (ntile * dilation, per_tile, w)
        grid = (ntile * nblk, dilation)
        slab = lambda i, r: (i // nblk) * dilation + r

        def cen(w):
            return pl.BlockSpec((1, LQ, w), lambda i, r: (slab(i, r), i % nblk, 0))

        def halo_spec(side):
            def index(i, r):
                j = (i % nblk) * hb + (-1 if side < 0 else hb)
                t = i // nblk + jnp.where(j < 0, -1, 0) + jnp.where(j >= nhalo, 1, 0)
                t = jnp.clip(t, 0, ntile - 1)
                return (t * dilation + r, j % nhalo, 0)
            return pl.BlockSpec((1, HALO, aw), index)
    else:
        tq = LQ // per_tile
        th = HALO // per_tile
        nhb = ntile // th
        shape = lambda w: (ntile, dilation, per_tile, w)
        grid = (ntile // tq, dilation)

        def cen(w):
            return pl.BlockSpec((tq, None, per_tile, w), lambda i, r: (i, r, 0, 0))

        def halo_spec(side):
            def index(i, r):
                j = i * (tq // th) + (-1 if side < 0 else tq // th)
                return (jnp.clip(j, 0, nhb - 1), r, 0, 0)
            return pl.BlockSpec((th, None, per_tile, aw), index)

    left, right = halo_spec(-1), halo_spec(1)
    view = lambda a: a.reshape(shape(a.shape[-1]))
    o, lse = pl.pallas_call(
        functools.partial(_attn_kernel, dilation=dilation, groups=groups),
        grid=grid,
        in_specs=[cen(aw), left, cen(aw), right, left, cen(aw), right],
        out_specs=[cen(aw), cen(LANES)],
        out_shape=[jax.ShapeDtypeStruct(shape(aw), BF16), jax.ShapeDtypeStruct(shape(LANES), F32)],
        compiler_params=_params("parallel", "parallel"),
        name=f"attn_d{dilation}",
    )(view(q), view(k), view(k), view(k), view(v), view(v), view(v))
    return o.reshape(T, aw), lse.reshape(T, LANES)


def _log_sigmoid(x):
    return jnp.minimum(x, 0.0) - jnp.log(1.0 + jnp.exp(-jnp.abs(x)))


def _mlstm_kernel(*refs, tb, reverse, first_pass, groups):
    nh, hd = MLSTM_HEADS, MLSTM_HEAD_DIM
    mw = MLSTM_WIDTH
    if first_pass:
        (qk_ref, qkp_ref, qkn_ref, cw_ref, cb_ref, v_ref, gc_ref, gr_ref, bc_ref, br_ref,
         h_ref, qo_ref, ko_ref, c_scr, m_scr, q_scr, k_scr, xe_scr) = refs
    else:
        (q_ref, k_ref, v_ref, gc_ref, gr_ref, bc_ref, br_ref, hf_ref, om_ref, og_ref,
         h_ref, c_scr, m_scr) = refs
    step = pl.program_id(0)
    blk = pl.num_programs(0) - 1 - step if reverse else step
    row0 = blk * tb
    start, end = _seq_bounds(row0, groups)
    at_start = row0 == start
    at_end = row0 + tb == end

    @pl.when(at_end if reverse else at_start)
    def _():
        c_scr[...] = jnp.zeros_like(c_scr)
        m_scr[...] = jnp.zeros_like(m_scr)

    if first_pass:
        pad = CONV_WIDTH // 2
        xe_scr[0:SUBLANES, :] = jnp.where(at_start, 0.0, qkp_ref[...])
        xe_scr[SUBLANES:SUBLANES + tb, :] = qk_ref[...]
        xe_scr[SUBLANES + tb:, :] = jnp.where(at_end, 0.0, qkn_ref[...])
        acc = jnp.zeros((tb, 2 * mw), F32) + cb_ref[...]
        for j in range(CONV_WIDTH):
            acc = acc + xe_scr[SUBLANES - pad + j:SUBLANES - pad + j + tb, :] * cw_ref[j:j + 1, :]
        act = acc * jax.nn.sigmoid(acc)
        qb = act[:, :mw].astype(BF16)
        kb = (act[:, mw:] * (hd ** -0.5)).astype(BF16)
        qo_ref[...] = qb
        ko_ref[...] = kb
        q_scr[...] = qb
        k_scr[...] = kb
        q_src, k_src = q_scr, k_scr
    else:
        q_src, k_src = q_ref, k_ref

    ti = lax.broadcasted_iota(jnp.int32, (CHUNK, CHUNK), 0)
    si = lax.broadcasted_iota(jnp.int32, (CHUNK, CHUNK), 1)
    causal = (si >= ti) if reverse else (si <= ti)
    cum_col = causal.astype(BF16)
    cum_row = ((ti >= si) if reverse else (ti <= si)).astype(BF16)
    last = 0 if reverse else CHUNK - 1
    ones_col = (lax.broadcasted_iota(jnp.int32, (CHUNK, hd), 1) == 0).astype(BF16)
    ngate = 2 * nh
    gc_is_i = lax.broadcasted_iota(jnp.int32, (CHUNK, ngate), 1) < nh
    gr_is_i = lax.broadcasted_iota(jnp.int32, (ngate, CHUNK), 0) < nh
    nchunk = tb // CHUNK

    def chunk_body(j, carry):
        jj = nchunk - 1 - j if reverse else j
        r0 = pl.multiple_of(jj * CHUNK, CHUNK)
        rows = pl.ds(r0, CHUNK)
        a_c = gc_ref[rows, :] + bc_ref[...]
        a_r = gr_ref[jj] + br_ref[...]
        gl_c = jnp.where(gc_is_i, a_c, _log_sigmoid(a_c))
        gl_r = jnp.where(gr_is_i, a_r, _log_sigmoid(a_r))
        b_c = _dot_split_rhs(cum_col, gl_c, 3)
        b_r = _dot_split(gl_r, cum_row, 3)
        qc = q_src[rows, :]
        kc = k_src[rows, :]
        vc = v_ref[rows, :]
        if not first_pass:
            hfc = hf_ref[rows, :]
            omc = om_ref[rows, :]
        for h in range(nh):
            hs = slice(h * hd, (h + 1) * hd)
            bc = b_c[:, nh + h:nh + h + 1]
            br = b_r[nh + h:nh + h + 1, :]
            lir = gl_r[h:h + 1, :]
            lic = gl_c[:, h:h + 1]
            m_prev = m_scr[h:h + 1, 0:1]
            dmat = jnp.where(causal, bc - br + lir, NEG_INF)
            inter = bc + m_prev
            m_t = jnp.maximum(inter, jnp.max(dmat, axis=-1, keepdims=True))
            w_intra = jnp.exp(dmat - m_t)
            w_inter = jnp.exp(inter - m_t)
            qh, kh, vh = qc[:, hs], kc[:, hs], vc[:, hs]
            s = lax.dot_general(qh, kh, NT_DIMS, preferred_element_type=F32) * w_intra
            v_aug = jnp.concatenate([vh, ones_col], axis=1)
            c_old = c_scr[h]
            h_aug = (jnp.dot(s.astype(BF16), v_aug, preferred_element_type=F32)
                     + w_inter * jnp.dot(qh, c_old.astype(BF16), preferred_element_type=F32))
            num = h_aug[:, :hd]
            den = h_aug[:, hd:hd + 1]
            h_out = num / jnp.maximum(jnp.abs(den), jnp.exp(-m_t))
            b_last = bc[last:last + 1, :]
            log_w = b_last - bc + lic
            m_new = jnp.maximum(b_last + m_prev, jnp.max(log_w, axis=0, keepdims=True))
            ws = jnp.exp(log_w - m_new)
            keep = jnp.exp(b_last + m_prev - m_new)
            wv = (ws * v_aug.astype(F32)).astype(BF16)
            c_scr[h] = keep * c_old + lax.dot_general(kh, wv, TN_DIMS, preferred_element_type=F32)
            m_scr[h:h + 1, :] = jnp.broadcast_to(m_new, (1, LANES))
            if first_pass:
                h_ref[rows, hs] = h_out
            else:
                y = jax.nn.sigmoid(omc[:, hs]) * (hfc[:, hs] + h_out)
                y = y * lax.rsqrt(jnp.mean(y * y, axis=-1, keepdims=True) + RMS_EPS) * og_ref[:, hs]
                h_ref[rows, hs] = y
        return carry

    lax.fori_loop(0, nchunk, chunk_body, 0)


def _mlstm_pass(reverse, first_pass, operands, groups):
    mw, nh, hd = MLSTM_WIDTH, MLSTM_HEADS, MLSTM_HEAD_DIM
    tb = TILE
    T = operands[0].shape[0]
    nblk = T // tb
    ngate = 2 * nh
    bidx = (lambda s: nblk - 1 - s) if reverse else (lambda s: s)
    rows = lambda w: pl.BlockSpec((tb, w), lambda s: (bidx(s), 0))
    const = lambda a: pl.BlockSpec(a.shape, lambda s: (0,) * a.ndim)
    gate_rows = pl.BlockSpec((tb // CHUNK, ngate, CHUNK), lambda s: (bidx(s), 0, 0))
    hb = tb // SUBLANES
    nhb = T // SUBLANES
    state = [pltpu.VMEM((nh, hd, 2 * hd), F32), pltpu.VMEM((SUBLANES, LANES), F32)]
    if first_pass:
        qk, cw, cb, v, gc, gr, bc, br = operands
        prev = pl.BlockSpec((SUBLANES, 2 * mw), lambda s: (jnp.maximum(bidx(s) * hb - 1, 0), 0))
        nxt = pl.BlockSpec((SUBLANES, 2 * mw), lambda s: (jnp.minimum((bidx(s) + 1) * hb, nhb - 1), 0))
        in_specs = [rows(2 * mw), prev, nxt, const(cw), const(cb), rows(mw), rows(ngate), gate_rows,
                    const(bc), const(br)]
        args = (qk, qk, qk, cw, cb, v, gc, gr, bc, br)
        out_specs = [rows(mw), rows(mw), rows(mw)]
        out_shape = [jax.ShapeDtypeStruct((T, mw), F32), jax.ShapeDtypeStruct((T, mw), BF16),
                     jax.ShapeDtypeStruct((T, mw), BF16)]
        scratch = state + [pltpu.VMEM((tb, mw), BF16), pltpu.VMEM((tb, mw), BF16),
                           pltpu.VMEM((tb + 2 * SUBLANES, 2 * mw), F32)]
    else:
        q, k, v, gc, gr, bc, br, hf, om, og = operands
        in_specs = [rows(mw), rows(mw), rows(mw), rows(ngate), gate_rows, const(bc), const(br),
                    rows(mw), rows(mw), const(og)]
        args = operands
        out_specs = rows(mw)
        out_shape = jax.ShapeDtypeStruct((T, mw), F32)
        scratch = state
    return pl.pallas_call(
        functools.partial(_mlstm_kernel, tb=tb, reverse=reverse, first_pass=first_pass, groups=groups),
        grid=(nblk,),
        in_specs=in_specs,
        out_specs=out_specs,
        out_shape=out_shape,
        scratch_shapes=scratch,
        compiler_params=_params("arbitrary"),
        name="mlstm_fwd" if first_pass else "mlstm_bwd",
    )(*args)


def _outproj_kernel(o1_ref, o4_ref, o16_ref, l1_ref, l4_ref, l16_ref, ml_ref, x_ref, ag_ref, bd_ref,
                    p4t_ref, p16t_ref, ex_ref, w_ref, y_ref):
    unperm = lambda p_ref, o_ref: jnp.dot(p_ref[...], o_ref[...], preferred_element_type=F32)
    o1 = o1_ref[...].astype(F32)
    o4 = unperm(p4t_ref, o4_ref)
    o16 = unperm(p16t_ref, o16_ref)

    def unperm_f32(p_ref, l_ref):
        return _dot_split_rhs(p_ref[...], l_ref[...], 2)

    l1 = l1_ref[...]
    l4 = unperm_f32(p4t_ref, l4_ref)
    l16 = unperm_f32(p16t_ref, l16_ref)
    mx = jnp.maximum(jnp.maximum(l1, l4), l16)
    e1, e4, e16 = jnp.exp(l1 - mx), jnp.exp(l4 - mx), jnp.exp(l16 - mx)
    inv = 1.0 / (e1 + e4 + e16)
    expand = lambda w: _dot_split(w, ex_ref[...], 2)
    attn = o1 + expand(e4 * inv) * (o4 - o1) + expand(e16 * inv) * (o16 - o1)
    ss = _group_sumsq(attn, bd_ref)
    attn = attn * lax.rsqrt(ss * (1.0 / ATTN_HEAD_DIM) + RMS_EPS) * ag_ref[...]
    aw = ATTN_WIDTH
    y = jnp.dot(attn.astype(BF16), w_ref[:aw, :], preferred_element_type=F32)
    y = y + jnp.dot(ml_ref[...].astype(BF16), w_ref[aw:, :], preferred_element_type=F32)
    y_ref[...] = x_ref[...] + y


def _outproj(os_, ls_, ml, x2, lw, consts):
    T = x2.shape[0]
    tm = TILE
    row = lambda w: pl.BlockSpec((tm, w), lambda i: (i, 0))
    const = lambda a: pl.BlockSpec(a.shape, lambda i: (0, 0))
    aw = ATTN_WIDTH
    cs = (lw['ag'], consts['bd'], consts['p4t'], consts['p16t'], consts['expand'], lw['w_out'])
    return pl.pallas_call(
        _outproj_kernel,
        grid=(T // tm,),
        in_specs=[row(aw)] * 3 + [row(LANES)] * 3 + [row(MLSTM_WIDTH), row(D_MODEL)] + [const(a) for a in cs],
        out_specs=row(D_MODEL),
        out_shape=jax.ShapeDtypeStruct((T, D_MODEL), F32),
        compiler_params=_params("parallel"),
        name="outproj",
    )(*os_, *ls_, ml, x2, *cs)


def _router_kernel(x_ref, g_ref, w_ref, b_ref, h_ref, lg_ref):
    x = x_ref[...]
    h = x * lax.rsqrt(jnp.mean(x * x, axis=-1, keepdims=True) + RMS_EPS) * g_ref[...]
    h_ref[...] = h.astype(BF16)
    lg_ref[...] = _dot_f32(h, w_ref[...]) + b_ref[...]


def _router(x2, g2, rw, rb):
    T = x2.shape[0]
    tm = TILE
    row = lambda w: pl.BlockSpec((tm, w), lambda i: (i, 0))
    const = lambda a: pl.BlockSpec(a.shape, lambda i: (0, 0))
    return pl.pallas_call(
        _router_kernel,
        grid=(T // tm,),
        in_specs=[row(D_MODEL), const(g2), const(rw), const(rb)],
        out_specs=[row(D_MODEL), row(N_EXPERTS)],
        out_shape=[jax.ShapeDtypeStruct((T, D_MODEL), BF16), jax.ShapeDtypeStruct((T, N_EXPERTS), F32)],
        compiler_params=_params("parallel"),
        name="router",
    )(x2, g2, rw, rb)


def _expert_kernel(blk_ref, exp_ref, lo_ref, hi_ref, x_ref, wu_ref, bu_ref, wd_ref, bd_ref, y_ref,
                   wu_scr, wd_scr, *, bm):
    v = pl.program_id(0)
    pv = jnp.maximum(v - 1, 0)
    new_expert = (v == 0) | (exp_ref[v] != exp_ref[pv])
    first_visit = (v == 0) | (blk_ref[v] != blk_ref[pv])
    cast_rows = 128

    @pl.when(new_expert)
    def _():
        def cast(i, c):
            r = pl.ds(pl.multiple_of(i * cast_rows, cast_rows), cast_rows)
            wu_scr[r, :] = wu_ref[0, 0, r, :].astype(BF16)
            wd_scr[r, :] = wd_ref[0, 0, r, :].astype(BF16)
            return c
        lax.fori_loop(0, D_MODEL // cast_rows, cast, 0)

    lo, hi = lo_ref[v], hi_ref[v]

    @pl.when(hi > lo)
    def _():
        x = x_ref[...]
        half = D_FF // 2
        y = jnp.zeros((bm, D_MODEL), F32) + bd_ref[0, 0]
        for c in range(2):
            g0, l0 = c * half, D_FF + c * half
            x_glu = jnp.dot(x, wu_scr[:, g0:g0 + half], preferred_element_type=F32) + bu_ref[0, 0, :, g0:g0 + half]
            x_lin = jnp.dot(x, wu_scr[:, l0:l0 + half], preferred_element_type=F32) + bu_ref[0, 0, :, l0:l0 + half]
            x_glu = jnp.minimum(x_glu, SWIGLU_LIMIT)
            x_lin = jnp.clip(x_lin, -SWIGLU_LIMIT, SWIGLU_LIMIT)
            act = x_glu * jax.nn.sigmoid(SWIGLU_ALPHA * x_glu) * (x_lin + 1.0)
            y = y + jnp.dot(act.astype(BF16), wd_scr[g0:g0 + half, :], preferred_element_type=F32)
        rows = blk_ref[v] * bm + lax.broadcasted_iota(jnp.int32, (bm, 1), 0)
        mine = (rows >= lo) & (rows < hi)

        @pl.when(first_visit)
        def _():
            y_ref[...] = jnp.where(mine, y, 0.0)

        @pl.when(jnp.logical_not(first_visit))
        def _():
            y_ref[...] = jnp.where(mine, y, y_ref[...])


def _experts(meta, x_sorted, w_up, b_up, w_down, b_down, layer):
    A = x_sorted.shape[0]
    bm = EXPERT_ROWS
    nvisit = meta[0].shape[0]
    grid_spec = pltpu.PrefetchScalarGridSpec(
        num_scalar_prefetch=4,
        grid=(nvisit,),
        in_specs=[pl.BlockSpec((bm, D_MODEL), lambda v, blk, ex, lo, hi: (blk[v], 0)),
                  pl.BlockSpec((1, 1, D_MODEL, 2 * D_FF), lambda v, blk, ex, lo, hi: (layer, ex[v], 0, 0)),
                  pl.BlockSpec((1, 1, 1, 2 * D_FF), lambda v, blk, ex, lo, hi: (layer, ex[v], 0, 0)),
                  pl.BlockSpec((1, 1, D_FF, D_MODEL), lambda v, blk, ex, lo, hi: (layer, ex[v], 0, 0)),
                  pl.BlockSpec((1, 1, 1, D_MODEL), lambda v, blk, ex, lo, hi: (layer, ex[v], 0, 0))],
        out_specs=pl.BlockSpec((bm, D_MODEL), lambda v, blk, ex, lo, hi: (blk[v], 0)),
        scratch_shapes=[pltpu.VMEM((D_MODEL, 2 * D_FF), BF16), pltpu.VMEM((D_FF, D_MODEL), BF16)],
    )
    return pl.pallas_call(
        functools.partial(_expert_kernel, bm=bm),
        grid_spec=grid_spec,
        out_shape=jax.ShapeDtypeStruct((A, D_MODEL), F32),
        compiler_params=_params("arbitrary", vmem=VMEM_LIMIT_EXPERTS),
        name="experts",
    )(*meta, x_sorted, w_up, b_up[:, :, None, :], w_down, b_down[:, :, None, :])


def _combine_kernel(x_ref, y0_ref, y1_ref, y2_ref, y3_ref, g_ref, o_ref):
    acc = x_ref[...]
    g = g_ref[...]
    for k, y_ref in enumerate((y0_ref, y1_ref, y2_ref, y3_ref)):
        acc = acc + y_ref[...] * g[:, k:k + 1]
    o_ref[...] = acc


def _combine(x2, ys, gates):
    T = x2.shape[0]
    tm = TILE
    row = lambda w: pl.BlockSpec((tm, w), lambda i: (i, 0))
    return pl.pallas_call(
        _combine_kernel,
        grid=(T // tm,),
        in_specs=[row(D_MODEL)] * 5 + [row(TOP_K)],
        out_specs=row(D_MODEL),
        out_shape=jax.ShapeDtypeStruct((T, D_MODEL), F32),
        compiler_params=_params("parallel"),
        name="combine",
    )(x2, *ys, gates)


def _visit_schedule(counts, nblk, bm):
    ends = jnp.cumsum(counts)
    starts = ends - counts
    first_blk = starts // bm
    nvis = jnp.where(counts > 0, (ends - 1) // bm - first_blk + 1, 0)
    vis_end = jnp.cumsum(nvis)
    vis_start = vis_end - nvis
    total = vis_end[-1]
    v = jnp.arange(nblk + N_EXPERTS - 1, dtype=jnp.int32)
    e = jnp.clip(jnp.searchsorted(vis_end, v, side='right'), 0, N_EXPERTS - 1).astype(jnp.int32)
    valid = v < total
    e_last = jnp.max(jnp.where(counts > 0, jnp.arange(N_EXPERTS, dtype=jnp.int32), 0))
    e = jnp.where(valid, e, e_last)
    blk = jnp.where(valid, first_blk[e] + v - vis_start[e], nblk - 1)
    lo = jnp.where(valid, starts[e], 0)
    hi = jnp.where(valid, ends[e], 0)
    i32 = lambda a: a.astype(jnp.int32)
    return i32(blk), i32(e), i32(lo), i32(hi)


def _moe(x2, lw, w_up, b_up, w_down, b_down, layer):
    T = x2.shape[0]
    A = T * TOP_K
    bm = EXPERT_ROWS
    assert A % bm == 0
    h, logits = _router(x2, lw['g2'], lw['rw'], lw['rb'])
    top_val, top_idx = lax.top_k(logits, TOP_K)
    gates = jax.nn.softmax(top_val, axis=-1)
    flat_e = top_idx.reshape(-1).astype(jnp.int32)
    iota = jnp.arange(A, dtype=jnp.int32)
    _, order = lax.sort((flat_e, iota), num_keys=1, is_stable=True)
    _, inv = lax.sort((order, iota), num_keys=1)
    counts = jnp.sum((flat_e[:, None] == jnp.arange(N_EXPERTS, dtype=jnp.int32)[None, :]).astype(jnp.int32),
                     axis=0)
    x_sorted = jnp.take(h, order // TOP_K, axis=0, mode='clip')
    meta = _visit_schedule(counts, A // bm, bm)
    y_sorted = _experts(meta, x_sorted, w_up, b_up, w_down, b_down, layer)
    inv4 = inv.reshape(T, TOP_K)
    ys = [jnp.take(y_sorted, inv4[:, k], axis=0, mode='clip') for k in range(TOP_K)]
    return _combine(x2, ys, gates)


def _rotary_tables(seq):
    half = ROT_DIM // 2
    inv_freq = ROPE_THETA ** (-jnp.arange(0, ROT_DIM, 2, dtype=F32) / ROT_DIM)
    ang = jnp.arange(seq).astype(F32)[:, None] * inv_freq[None, :]
    cos, sin = jnp.cos(ang), jnp.sin(ang)
    rest = ATTN_HEAD_DIM - ROT_DIM
    ones = jnp.ones((seq, rest), F32)
    zeros = jnp.zeros((seq, rest), F32)
    zh = jnp.zeros((seq, half), F32)
    cos_a = jnp.concatenate([cos, cos, ones], axis=1)
    sin_b = jnp.concatenate([-sin, zh, zeros], axis=1)
    sin_c = jnp.concatenate([zh, sin, zeros], axis=1)
    tile = lambda a: jnp.tile(a, (1, LANES // ATTN_HEAD_DIM))
    return tile(cos_a), tile(sin_b), tile(sin_c)


def _constants():
    idx = jnp.arange(ATTN_WIDTH) // ATTN_HEAD_DIM
    bd = (idx[:, None] == idx[None, :]).astype(BF16)

    def perm(d):
        new = jnp.arange(TILE)
        old = (new % (TILE // d)) * d + new // (TILE // d)
        return (old[:, None] == jnp.arange(TILE)[None, :]).astype(BF16)

    p4, p16 = perm(4), perm(16)
    expand = (jnp.arange(LANES)[:, None] == idx[None, :]).astype(BF16)
    return dict(bd=bd, p4=p4, p16=p16, p4t=p4.T, p16t=p16.T, expand=expand)


def _layer_weights(l, norm1_g, w_in, q_norm_g, k_norm_g, attn_out_g, conv_w, conv_b, igate_b, fgate_b,
                   mlstm_out_g, w_out, norm2_g, router_w, router_b):
    aw, mw, nh = ATTN_WIDTH, MLSTM_WIDTH, MLSTM_HEADS
    n_main = 3 * aw + 4 * mw
    w = w_in[l]
    wg = w[:, n_main:]
    pick = lambda a, d: jnp.concatenate([a[..., d * nh:(d + 1) * nh],
                                         a[..., (2 + d) * nh:(3 + d) * nh]], axis=-1)
    gate_bias = [jnp.concatenate([igate_b[l, d], fgate_b[l, d]]) for d in range(2)]
    return dict(
        g1=norm1_g[l][None, :], w_main=w[:, :n_main].astype(BF16),
        w_gate=jnp.concatenate([pick(wg, 0), pick(wg, 1)], axis=1),
        qg=jnp.tile(q_norm_g[l], ATTN_HEADS)[None, :], kg=jnp.tile(k_norm_g[l], ATTN_HEADS)[None, :],
        ag=attn_out_g[l][None, :], cw=conv_w[l], cb=conv_b[l][None, :],
        bias_c=[b[None, :] for b in gate_bias], bias_r=[b[:, None] for b in gate_bias],
        og=mlstm_out_g[l][None, :], w_out=w_out[l].astype(BF16), g2=norm2_g[l][None, :],
        rw=router_w[l], rb=router_b[l][None, :])


def kernel(x_prompt, x_sample, norm1_g, w_in, q_norm_g, k_norm_g, attn_out_g, conv_w, conv_b, igate_b,
           fgate_b, mlstm_out_g, w_out, norm2_g, router_w, router_b, w_up, b_up, w_down, b_down):
    assert all(w // (2 * d) == HALO for w, d in SEGMENTS)
    groups = (x_prompt.shape[:2], x_sample.shape[:2])
    span = max(d for _, d in SEGMENTS) * LQ
    assert all(s % span == 0 for _, s in groups), groups
    small = (norm1_g, w_in, q_norm_g, k_norm_g, attn_out_g, conv_w, conv_b, igate_b, fgate_b,
             mlstm_out_g, w_out, norm2_g, router_w, router_b)
    consts = _constants()
    tables = _rotary_tables(max(s for _, s in groups))
    x2 = jnp.concatenate([x_prompt.reshape(-1, D_MODEL), x_sample.reshape(-1, D_MODEL)], axis=0)
    T = x2.shape[0]
    ngate = 2 * MLSTM_HEADS
    chunk_rows = lambda g: jnp.swapaxes(g.reshape(T // CHUNK, CHUNK, ngate), 1, 2)
    for l in range(DEPTH):
        lw = _layer_weights(l, *small)
        (q1, k1, v1, q4, k4, v4, q16, k16, v16, qkm, vm, om, gf, gb) = _inproj(x2, lw, tables, consts, groups)
        segs = [_attn_segment(q1, k1, v1, 1, groups), _attn_segment(q4, k4, v4, 4, groups),
                _attn_segment(q16, k16, v16, 16, groups)]
        hf, qm, km = _mlstm_pass(False, True, (qkm, lw['cw'], lw['cb'], vm, gf, chunk_rows(gf),
                                               lw['bias_c'][0], lw['bias_r'][0]), groups)
        ml = _mlstm_pass(True, False, (qm, km, vm, gb, chunk_rows(gb), lw['bias_c'][1], lw['bias_r'][1],
                                       hf, om, lw['og']), groups)
        x2 = _outproj([s[0] for s in segs], [s[1] for s in segs], ml, x2, lw, consts)
        x2 = _moe(x2, lw, w_up, b_up, w_down, b_down, l)
    t1 = x_prompt.shape[0] * x_prompt.shape[1]
    return (x2[:t1].reshape(x_prompt.shape), x2[t1:].reshape(x_sample.shape))
```

```python
import functools

import jax
import jax.numpy as jnp
from jax import lax
from jax.experimental import pallas as pl
from jax.experimental.pallas import tpu as pltpu

D_MODEL = 1024
DEPTH = 2
ATTN_WIDTH = 512
ATTN_HEAD_DIM = 64
ATTN_HEADS = 8
ROT_DIM = 16
ROPE_THETA = 500000.0
SEGMENTS = ((128, 1), (512, 4), (2048, 16))
HALO = 64
MLSTM_WIDTH = 512
MLSTM_HEAD_DIM = 128
MLSTM_HEADS = 4
CONV_WIDTH = 5
CHUNK = 64
N_EXPERTS = 32
TOP_K = 4
D_FF = 1024
SWIGLU_LIMIT = 7.0
SWIGLU_ALPHA = 1.702
RMS_EPS = 1e-6
NEG_INF = -1e30

LANES = 128
SUBLANES = 8
TILE = 512
LQ = 128
EXPERT_ROWS = 512
VMEM_LIMIT = 48 * 1024 * 1024
VMEM_LIMIT_EXPERTS = 56 * 1024 * 1024

F32 = jnp.float32
BF16 = jnp.bfloat16
NT_DIMS = (((1,), (1,)), ((), ()))
TN_DIMS = (((0,), (0,)), ((), ()))


def _params(*sem, vmem=VMEM_LIMIT):
    return pltpu.CompilerParams(dimension_semantics=sem, vmem_limit_bytes=vmem)


def _seq_bounds(row, groups):
    (b1, s1), (_, s2) = groups
    t1 = b1 * s1
    in1 = row < t1
    start = jnp.where(in1, (row // s1) * s1, t1 + ((row - t1) // s2) * s2)
    return start, start + jnp.where(in1, s1, s2)


def _split3(a):
    hi = a.astype(BF16)
    r1 = a - hi.astype(F32)
    mid = r1.astype(BF16)
    lo = (r1 - mid.astype(F32)).astype(BF16)
    return hi, mid, lo


def _dot_split(a, b, pieces):
    parts = _split3(a)[:pieces]
    out = jnp.dot(parts[0], b, preferred_element_type=F32)
    for p in parts[1:]:
        out = out + jnp.dot(p, b, preferred_element_type=F32)
    return out


def _dot_split_rhs(a, b, pieces):
    parts = _split3(b)[:pieces]
    out = jnp.dot(a, parts[0], preferred_element_type=F32)
    for p in parts[1:]:
        out = out + jnp.dot(a, p, preferred_element_type=F32)
    return out


def _dot_f32(a, b):
    ah, al, _ = _split3(a)
    bh, bl, _ = _split3(b)
    out = jnp.dot(ah, bh, preferred_element_type=F32)
    out = out + jnp.dot(ah, bl, preferred_element_type=F32)
    return out + jnp.dot(al, bh, preferred_element_type=F32)


def _group_sumsq(z, bd_ref):
    return _dot_split(z * z, bd_ref[...], 1)


def _inproj_kernel(x_ref, g1_ref, w_ref, wg_ref, qg_ref, kg_ref, cos_ref, sinb_ref, sinc_ref,
                   bd_ref, p4_ref, p16_ref,
                   q1_ref, k1_ref, v1_ref, q4_ref, k4_ref, v4_ref, q16_ref, k16_ref, v16_ref,
                   qkm_ref, vm_ref, om_ref, gf_ref, gb_ref):
    x = x_ref[...]
    xn = x * lax.rsqrt(jnp.mean(x * x, axis=-1, keepdims=True) + RMS_EPS) * g1_ref[...]
    xb = xn.astype(BF16)
    aw, mw = ATTN_WIDTH, MLSTM_WIDTH

    def proj(lo, hi):
        return jnp.dot(xb, w_ref[:, lo:hi], preferred_element_type=F32)

    reps = aw // LANES
    cos_a = jnp.concatenate([cos_ref[...]] * reps, axis=1)
    sin_b = jnp.concatenate([sinb_ref[...]] * reps, axis=1)
    sin_c = jnp.concatenate([sinc_ref[...]] * reps, axis=1)

    def norm_rot(z, g_ref):
        ss = _group_sumsq(z, bd_ref)
        y = z * lax.rsqrt(ss * (1.0 / ATTN_HEAD_DIM) + RMS_EPS) * g_ref[...]
        half = ROT_DIM // 2
        y_up = pltpu.roll(y, aw - half, axis=1)
        y_dn = pltpu.roll(y, half, axis=1)
        return (y * cos_a + y_up * sin_b + y_dn * sin_c).astype(BF16)

    def emit(z, nat_ref, r4_ref, r16_ref):
        nat_ref[...] = z
        r4_ref[...] = jnp.dot(p4_ref[...], z, preferred_element_type=F32).astype(BF16)
        r16_ref[...] = jnp.dot(p16_ref[...], z, preferred_element_type=F32).astype(BF16)

    emit(norm_rot(proj(0, aw), qg_ref), q1_ref, q4_ref, q16_ref)
    emit(norm_rot(proj(aw, 2 * aw), kg_ref), k1_ref, k4_ref, k16_ref)
    emit(proj(2 * aw, 3 * aw).astype(BF16), v1_ref, v4_ref, v16_ref)
    qkm_ref[...] = proj(3 * aw, 3 * aw + 2 * mw)
    vm_ref[...] = proj(3 * aw + 2 * mw, 3 * aw + 3 * mw).astype(BF16)
    om_ref[...] = proj(3 * aw + 3 * mw, 3 * aw + 4 * mw)
    gates = _dot_f32(xn, wg_ref[...])
    ng = 2 * MLSTM_HEADS
    gf_ref[...] = gates[:, :ng]
    gb_ref[...] = gates[:, ng:]


def _inproj(x2, lw, tables, consts, groups):
    T = x2.shape[0]
    tm = TILE
    aw, mw, ng = ATTN_WIDTH, MLSTM_WIDTH, 2 * MLSTM_HEADS
    row = lambda w: pl.BlockSpec((tm, w), lambda i: (i, 0))
    const = lambda a: pl.BlockSpec(a.shape, lambda i: (0, 0))

    def tab_index(i):
        start, _ = _seq_bounds(i * tm, groups)
        return (i - start // tm, 0)

    tab = pl.BlockSpec((tm, LANES), tab_index)
    weights = (lw['g1'], lw['w_main'], lw['w_gate'], lw['qg'], lw['kg'])
    perms = (consts['bd'], consts['p4'], consts['p16'])
    bf = lambda w: jax.ShapeDtypeStruct((T, w), BF16)
    f32 = lambda w: jax.ShapeDtypeStruct((T, w), F32)
    return pl.pallas_call(
        _inproj_kernel,
        grid=(T // tm,),
        in_specs=[row(D_MODEL)] + [const(a) for a in weights] + [tab, tab, tab] + [const(a) for a in perms],
        out_specs=[row(aw)] * 9 + [row(2 * mw), row(mw), row(mw), row(ng), row(ng)],
        out_shape=[bf(aw)] * 9 + [f32(2 * mw), bf(mw), f32(mw), f32(ng), f32(ng)],
        compiler_params=_params("parallel"),
        name="inproj",
    )(x2, *weights, *tables, *perms)


def _attn_kernel(q_ref, kl_ref, kc_ref, kr_ref, vl_ref, vc_ref, vr_ref, o_ref, lse_ref, *,
                 dilation, groups):
    aw = ATTN_WIDTH
    lq, lk = LQ, LQ + 2 * HALO
    row0 = pl.program_id(0) * (LQ * dilation)
    start, end = _seq_bounds(row0, groups)
    m0 = (row0 - start) // dilation
    sub_len = (end - start) // dilation

    flat = lambda ref, n: ref[...].reshape(n, ref.shape[-1])
    q = flat(q_ref, lq)
    k = jnp.concatenate([flat(kl_ref, HALO), flat(kc_ref, lq), flat(kr_ref, HALO)], axis=0)
    v = jnp.concatenate([flat(vl_ref, HALO), flat(vc_ref, lq), flat(vr_ref, HALO)], axis=0)

    row = lax.broadcasted_iota(jnp.int32, (2 * lq, lk), 0)
    col = lax.broadcasted_iota(jnp.int32, (2 * lq, lk), 1)
    qrow = jnp.where(row >= lq, row - lq, row)
    rel = col - HALO - qrow
    kpos = m0 - HALO + col
    mask = (jnp.abs(rel) <= HALO) & (kpos >= 0) & (kpos < sub_len)

    lane2 = lax.broadcasted_iota(jnp.int32, (2 * lq, LANES), 1)
    row2 = lax.broadcasted_iota(jnp.int32, (2 * lq, LANES), 0)
    own = (lane2 >= ATTN_HEAD_DIM) ^ (row2 < lq)
    lane = lax.broadcasted_iota(jnp.int32, (lq, LANES), 1)
    first = lane < ATTN_HEAD_DIM

    lse_tile =jnp.zeros((lq, LANES), F32)
    outs = []
    for hp in range(aw // LANES):
        sl = slice(hp * LANES, (hp + 1) * LANES)
        qp = q[:, sl]
        q2 = jnp.concatenate([qp, qp], axis=0)
        q2 = jnp.where(own, q2, jnp.zeros_like(q2))
        s = lax.dot_general(q2, k[:, sl], NT_DIMS, preferred_element_type=F32)
        s = jnp.where(mask, s, NEG_INF)
        mx = jnp.max(s, axis=-1, keepdims=True)
        p = jnp.exp(s - mx)
        den = jnp.sum(p, axis=-1, keepdims=True)
        pv = jnp.dot(p.astype(BF16), v[:, sl], preferred_element_type=F32)
        o2 = pv / den
        lse2 = mx + jnp.log(den)
        outs.append(jnp.where(first, o2[:lq], o2[lq:]).astype(BF16))
        lse_tile = jnp.where(lane == 2 * hp, lse2[:lq], lse_tile)
        lse_tile = jnp.where(lane == 2 * hp + 1, lse2[lq:], lse_tile)
    o_ref[...] = jnp.concatenate(outs, axis=1).reshape(o_ref.shape)
    lse_ref[...] = lse_tile.reshape(lse_ref.shape)


def _attn_segment(q, k, v, dilation, groups):
    T, aw = q.shape
    tm = TILE
    per_tile = tm // dilation
    ntile = T // tm
    if per_tile >= LQ:
        nblk = per_tile // LQ
        nhalo = per_tile // HALO
        hb = LQ // HALO
        shape = lambda w: (ntile * dilation, per_tile, w)
        grid = (ntile * nblk, dilation)
        slab = lambda i, r: (i // nblk) * dilation + r

        def cen(w):
            return pl.BlockSpec((1, LQ, w), lambda i, r: (slab(i, r), i % nblk, 0))

        def halo_spec(side):
            def index(i, r):
                j = (i % nblk) * hb + (-1 if side < 0 else hb)
                t = i // nblk + jnp.where(j < 0, -1, 0) + jnp.where(j >= nhalo, 1, 0)
                t = jnp.clip(t, 0, ntile - 1)
                return (t * dilation + r, j % nhalo, 0)
            return pl.BlockSpec((1, HALO, aw), index)
    else:
        tq = LQ // per_tile
        th = HALO // per_tile
        nhb = ntile // th
        shape = lambda w: (ntile, dilation, per_tile, w)
        grid = (ntile // tq, dilation)

        def cen(w):
            return pl.BlockSpec((tq, None, per_tile, w), lambda i, r: (i, r, 0, 0))

        def halo_spec(side):
            def index(i, r):
                j = i * (tq // th) + (-1 if side < 0 else tq // th)
                return (jnp.clip(j, 0, nhb - 1), r, 0, 0)
            return pl.BlockSpec((th, None, per_tile, aw), index)

    left, right = halo_spec(-1), halo_spec(1)
    view = lambda a: a.reshape(shape(a.shape[-1]))
    o, lse = pl.pallas_call(
        functools.partial(_attn_kernel, dilation=dilation, groups=groups),
        grid=grid,
        in_specs=[cen(aw), left, cen(aw), right, left, cen(aw), right],
        out_specs=[cen(aw), cen(LANES)],
        out_shape=[jax.ShapeDtypeStruct(shape(aw), BF16), jax.ShapeDtypeStruct(shape(LANES), F32)],
        compiler_params=_params("parallel", "parallel"),
        name=f"attn_d{dilation}",
    )(view(q), view(k), view(k), view(k), view(v), view(v), view(v))
    return o.reshape(T, aw), lse.reshape(T, LANES)


def _log_sigmoid(x):
    return jnp.minimum(x, 0.0) - jnp.log(1.0 + jnp.exp(-jnp.abs(x)))


def _mlstm_kernel(*refs, tb, reverse, first_pass, groups):
    nh, hd = MLSTM_HEADS, MLSTM_HEAD_DIM
    mw = MLSTM_WIDTH
    if first_pass:
        (qk_ref, qkp_ref, qkn_ref, cw_ref, cb_ref, v_ref, gc_ref, gr_ref, bc_ref, br_ref,
         h_ref, qo_ref, ko_ref, c_scr, m_scr, intra_scr, upd_scr, col_scr, one_scr,
         q_scr, k_scr, xe_scr) = refs
    else:
        (q_ref, k_ref, v_ref, gc_ref, gr_ref, bc_ref, br_ref, hf_ref, om_ref, og_ref,
         h_ref, c_scr, m_scr, intra_scr, upd_scr, col_scr, one_scr) = refs
    step = pl.program_id(0)
    blk = pl.num_programs(0) - 1 - step if reverse else step
    row0 = blk * tb
    start, end = _seq_bounds(row0, groups)
    at_start = row0 == start
    at_end = row0 + tb == end

    @pl.when(at_end if reverse else at_start)
    def _():
        c_scr[...] = jnp.zeros_like(c_scr)
        m_scr[...] = jnp.zeros_like(m_scr)

    if first_pass:
        pad = CONV_WIDTH // 2
        xe_scr[0:SUBLANES, :] = jnp.where(at_start, 0.0, qkp_ref[...])
        xe_scr[SUBLANES:SUBLANES + tb, :] = qk_ref[...]
        xe_scr[SUBLANES + tb:, :] = jnp.where(at_end, 0.0, qkn_ref[...])
        acc = jnp.zeros((tb, 2 * mw), F32) + cb_ref[...]
        for j in range(CONV_WIDTH):
            acc = acc + xe_scr[SUBLANES - pad + j:SUBLANES - pad + j + tb, :] * cw_ref[j:j + 1, :]
        act = acc * jax.nn.sigmoid(acc)
        qb = act[:, :mw].astype(BF16)
        kb = (act[:, mw:] * (hd ** -0.5)).astype(BF16)
        qo_ref[...] = qb
        ko_ref[...] = kb
        q_scr[...] = qb
        k_scr[...] = kb
        q_src, k_src = q_scr, k_scr
    else:
        q_src, k_src = q_ref, k_ref

    ti = lax.broadcasted_iota(jnp.int32, (CHUNK, CHUNK), 0)
    si = lax.broadcasted_iota(jnp.int32, (CHUNK, CHUNK), 1)
    causal = (si >= ti) if reverse else (si <= ti)
    cum_col = causal.astype(BF16)
    cum_row = ((ti >= si) if reverse else (ti <= si)).astype(BF16)
    last = 0 if reverse else CHUNK - 1
    ones_col = (lax.broadcasted_iota(jnp.int32, (CHUNK, hd), 1) == 0).astype(BF16)
    ngate = 2 * nh
    gc_is_i = lax.broadcasted_iota(jnp.int32, (CHUNK, ngate), 1) < nh
    gr_is_i = lax.broadcasted_iota(jnp.int32, (ngate, CHUNK), 0) < nh
    nchunk = tb // CHUNK

    def local_terms(j, carry):
        rows = pl.ds(j * CHUNK, CHUNK)
        a_c = gc_ref[rows, :] + bc_ref[...]
        a_r = gr_ref[j] + br_ref[...]
        gl_c = jnp.where(gc_is_i, a_c, _log_sigmoid(a_c))
        gl_r = jnp.where(gr_is_i, a_r, _log_sigmoid(a_r))
        b_c = _dot_split_rhs(cum_col, gl_c, 3)
        b_r = _dot_split(gl_r, cum_row, 3)
        qc = q_src[rows, :]
        kc = k_src[rows, :]
        vc = v_ref[rows, :]
        for h in range(nh):
            hs = slice(h * hd, (h + 1) * hd)
            bc = b_c[:, nh + h:nh + h + 1]
            br = b_r[nh + h:nh + h + 1, :]
            lir = gl_r[h:h + 1, :]
            lic = gl_c[:, h:h + 1]
            dmat = jnp.where(causal, bc - br + lir, NEG_INF)
            m_loc = jnp.max(dmat, axis=-1, keepdims=True)
            qh, kh, vh = qc[:, hs], kc[:, hs], vc[:, hs]
            s = lax.dot_general(qh, kh, NT_DIMS, preferred_element_type=F32) * jnp.exp(dmat - m_loc)
            v_aug = jnp.concatenate([vh, ones_col], axis=1)
            intra_scr[j, h] = jnp.dot(s.astype(BF16), v_aug, preferred_element_type=F32)
            b_last = bc[last:last + 1, :]
            log_w = b_last - bc + lic
            mw_loc = jnp.max(log_w, axis=0, keepdims=True)
            wv = (jnp.exp(log_w - mw_loc) * v_aug.astype(F32)).astype(BF16)
            upd_scr[j, h] = lax.dot_general(kh, wv, TN_DIMS, preferred_element_type=F32)
            col_scr[j, h, :, 0:1] = bc
            col_scr[j, h, :, 1:2] = m_loc
            one_scr[j, h, :, 0:1] = b_last
            one_scr[j, h, :, 1:2] = mw_loc
        return carry

    for j in range(nchunk):
        local_terms(j, 0)

    def recurrence(j, carry):
        jj = nchunk - 1 - j if reverse else j
        r0 = pl.multiple_of(jj * CHUNK, CHUNK)
        rows = pl.ds(r0, CHUNK)
        qc = q_src[rows, :]
        if not first_pass:
            hfc = hf_ref[rows, :]
            omc = om_ref[rows, :]
        for h in range(nh):
            hs = slice(h * hd, (h + 1) * hd)
            m_prev = m_scr[h:h + 1, 0:1]
            bc = col_scr[jj, h, :, 0:1]
            m_loc = col_scr[jj, h, :, 1:2]
            b_last = one_scr[jj, h, :, 0:1]
            mw_loc = one_scr[jj, h, :, 1:2]
            inter = bc + m_prev
            m_t = jnp.maximum(inter, m_loc)
            c_old = c_scr[h]
            h_aug = (jnp.exp(m_loc - m_t) * intra_scr[jj, h]
                     + jnp.exp(inter - m_t) * jnp.dot(qc[:, hs], c_old.astype(BF16), preferred_element_type=F32))
            num = h_aug[:, :hd]
            den = h_aug[:, hd:hd + 1]
            h_out = num / jnp.maximum(jnp.abs(den), jnp.exp(-m_t))
            m_new = jnp.maximum(b_last + m_prev, mw_loc)
            c_scr[h] = jnp.exp(b_last + m_prev - m_new) * c_old + jnp.exp(mw_loc - m_new) * upd_scr[jj, h]
            m_scr[h:h + 1, :] = jnp.broadcast_to(m_new, (1, LANES))
            if first_pass:
                h_ref[rows, hs] = h_out
            else:
                y = jax.nn.sigmoid(omc[:, hs]) * (hfc[:, hs] + h_out)
                y = y * lax.rsqrt(jnp.mean(y * y, axis=-1, keepdims=True) + RMS_EPS) * og_ref[:, hs]
                h_ref[rows, hs] = y
        return carry

    lax.fori_loop(0, nchunk, recurrence, 0)


def _mlstm_pass(reverse, first_pass, operands, groups):
    mw, nh, hd = MLSTM_WIDTH, MLSTM_HEADS, MLSTM_HEAD_DIM
    tb = TILE
    T = operands[0].shape[0]
    nblk = T // tb
    ngate = 2 * nh
    bidx = (lambda s: nblk - 1 - s) if reverse else (lambda s: s)
    rows = lambda w: pl.BlockSpec((tb, w), lambda s: (bidx(s), 0))
    const = lambda a: pl.BlockSpec(a.shape, lambda s: (0,) * a.ndim)
    gate_rows = pl.BlockSpec((tb // CHUNK, ngate, CHUNK), lambda s: (bidx(s), 0, 0))
    hb = tb // SUBLANES
    nhb = T // SUBLANES
    nchunk = tb // CHUNK
    state = [pltpu.VMEM((nh, hd, 2 * hd), F32), pltpu.VMEM((SUBLANES, LANES), F32),
             pltpu.VMEM((nchunk, nh, CHUNK, 2 * hd), F32), pltpu.VMEM((nchunk, nh, hd, 2 * hd), F32),
             pltpu.VMEM((nchunk, nh, CHUNK, 2), F32), pltpu.VMEM((nchunk, nh, 1, 2), F32)]
    if first_pass:
        qk, cw, cb, v, gc, gr, bc, br = operands
        prev = pl.BlockSpec((SUBLANES, 2 * mw), lambda s: (jnp.maximum(bidx(s) * hb - 1, 0), 0))
        nxt = pl.BlockSpec((SUBLANES, 2 * mw), lambda s: (jnp.minimum((bidx(s) + 1) * hb, nhb - 1), 0))
        in_specs = [rows(2 * mw), prev, nxt, const(cw), const(cb), rows(mw), rows(ngate), gate_rows,
                    const(bc), const(br)]
        args = (qk, qk, qk, cw, cb, v, gc, gr, bc, br)
        out_specs = [rows(mw), rows(mw), rows(mw)]
        out_shape = [jax.ShapeDtypeStruct((T, mw), F32), jax.ShapeDtypeStruct((T, mw), BF16),
                     jax.ShapeDtypeStruct((T, mw), BF16)]
        scratch = state + [pltpu.VMEM((tb, mw), BF16), pltpu.VMEM((tb, mw), BF16),
                           pltpu.VMEM((tb + 2 * SUBLANES, 2 * mw), F32)]
    else:
        q, k, v, gc, gr, bc, br, hf, om, og = operands
        in_specs = [rows(mw), rows(mw), rows(mw), rows(ngate), gate_rows, const(bc), const(br),
                    rows(mw), rows(mw), const(og)]
        args = operands
        out_specs = rows(mw)
        out_shape = jax.ShapeDtypeStruct((T, mw), F32)
        scratch = state
    return pl.pallas_call(
        functools.partial(_mlstm_kernel, tb=tb, reverse=reverse, first_pass=first_pass, groups=groups),
        grid=(nblk,),
        in_specs=in_specs,
        out_specs=out_specs,
        out_shape=out_shape,
        scratch_shapes=scratch,
        compiler_params=_params("arbitrary"),
        name="mlstm_fwd" if first_pass else "mlstm_bwd",
    )(*args)


def _outproj_kernel(o1_ref, o4_ref, o16_ref, l1_ref, l4_ref, l16_ref, ml_ref, x_ref, ag_ref, bd_ref,
                    p4t_ref, p16t_ref, ex_ref, w_ref, g2_ref, rw_ref, rb_ref, y_ref, h_ref, lg_ref):
    unperm = lambda p_ref, o_ref: jnp.dot(p_ref[...], o_ref[...], preferred_element_type=F32)
    o1 = o1_ref[...].astype(F32)
    o4 = unperm(p4t_ref, o4_ref)
    o16 = unperm(p16t_ref, o16_ref)

    def unperm_f32(p_ref, l_ref):
        return _dot_split_rhs(p_ref[...], l_ref[...], 2)

    l1 = l1_ref[...]
    l4 = unperm_f32(p4t_ref, l4_ref)
    l16 = unperm_f32(p16t_ref, l16_ref)
    mx = jnp.maximum(jnp.maximum(l1, l4), l16)
    e1, e4, e16 = jnp.exp(l1 - mx), jnp.exp(l4 - mx), jnp.exp(l16 - mx)
    inv = 1.0 / (e1 + e4 + e16)
    expand = lambda w: _dot_split(w, ex_ref[...], 2)
    attn = o1 + expand(e4 * inv) * (o4 - o1) + expand(e16 * inv) * (o16 - o1)
    ss = _group_sumsq(attn, bd_ref)
    attn = attn * lax.rsqrt(ss * (1.0 / ATTN_HEAD_DIM) + RMS_EPS) * ag_ref[...]
    aw = ATTN_WIDTH
    y = jnp.dot(attn.astype(BF16), w_ref[:aw, :], preferred_element_type=F32)
    y = y + jnp.dot(ml_ref[...].astype(BF16), w_ref[aw:, :], preferred_element_type=F32)
    x = x_ref[...] + y
    y_ref[...] = x
    h = x * lax.rsqrt(jnp.mean(x * x, axis=-1, keepdims=True) + RMS_EPS) * g2_ref[...]
    h_ref[...] = h.astype(BF16)
    lg_ref[...] = _dot_f32(h, rw_ref[...]) + rb_ref[...]


def _outproj(os_, ls_, ml, x2, lw, consts):
    T = x2.shape[0]
    tm = TILE
    row = lambda w: pl.BlockSpec((tm, w), lambda i: (i, 0))
    const = lambda a: pl.BlockSpec(a.shape, lambda i: (0, 0))
    aw = ATTN_WIDTH
    cs = (lw['ag'], consts['bd'], consts['p4t'], consts['p16t'], consts['expand'], lw['w_out'],
          lw['g2'], lw['rw'], lw['rb'])
    return pl.pallas_call(
        _outproj_kernel,
        grid=(T // tm,),
        in_specs=[row(aw)] * 3 + [row(LANES)] * 3 + [row(MLSTM_WIDTH), row(D_MODEL)] + [const(a) for a in cs],
        out_specs=[row(D_MODEL), row(D_MODEL), row(N_EXPERTS)],
        out_shape=[jax.ShapeDtypeStruct((T, D_MODEL), F32), jax.ShapeDtypeStruct((T, D_MODEL), BF16),
                   jax.ShapeDtypeStruct((T, N_EXPERTS), F32)],
        compiler_params=_params("parallel"),
        name="outproj",
    )(*os_, *ls_, ml, x2, *cs)


def _expert_kernel(blk_ref, exp_ref, lo_ref, hi_ref, x_ref, wu_ref, bu_ref, wd_ref, bd_ref, y_ref,
                   wu_scr, wd_scr, *, bm):
    v = pl.program_id(0)
    pv = jnp.maximum(v - 1, 0)
    new_expert = (v == 0) | (exp_ref[v] != exp_ref[pv])
    first_visit = (v == 0) | (blk_ref[v] != blk_ref[pv])
    cast_rows = 128

    @pl.when(new_expert)
    def _():
        def cast(i, c):
            r = pl.ds(pl.multiple_of(i * cast_rows, cast_rows), cast_rows)
            wu_scr[r, :] = wu_ref[0, 0, r, :].astype(BF16)
            wd_scr[r, :] = wd_ref[0, 0, r, :].astype(BF16)
            return c
        lax.fori_loop(0, D_MODEL // cast_rows, cast, 0)

    lo, hi = lo_ref[v], hi_ref[v]

    @pl.when(hi > lo)
    def _():
        x = x_ref[...]
        half = D_FF // 2
        y = jnp.zeros((bm, D_MODEL), F32) + bd_ref[0, 0]
        for c in range(2):
            g0, l0 = c * half, D_FF + c * half
            x_glu = jnp.dot(x, wu_scr[:, g0:g0 + half], preferred_element_type=F32) + bu_ref[0, 0, :, g0:g0 + half]
            x_lin = jnp.dot(x, wu_scr[:, l0:l0 + half], preferred_element_type=F32) + bu_ref[0, 0, :, l0:l0 + half]
            x_glu = jnp.minimum(x_glu, SWIGLU_LIMIT)
            x_lin = jnp.clip(x_lin, -SWIGLU_LIMIT, SWIGLU_LIMIT)
            act = x_glu * jax.nn.sigmoid(SWIGLU_ALPHA * x_glu) * (x_lin + 1.0)
            y = y + jnp.dot(act.astype(BF16), wd_scr[g0:g0 + half, :], preferred_element_type=F32)
        rows = blk_ref[v] * bm + lax.broadcasted_iota(jnp.int32, (bm, 1), 0)
        mine = (rows >= lo) & (rows < hi)

        @pl.when(first_visit)
        def _():
            y_ref[...] = jnp.where(mine, y, 0.0).astype(y_ref.dtype)

        @pl.when(jnp.logical_not(first_visit))
        def _():
            y_ref[...] = jnp.where(mine, y.astype(y_ref.dtype), y_ref[...])


def _experts(meta, x_sorted, w_up, b_up, w_down, b_down, layer):
    A = x_sorted.shape[0]
    bm = EXPERT_ROWS
    nvisit = meta[0].shape[0]
    grid_spec = pltpu.PrefetchScalarGridSpec(
        num_scalar_prefetch=4,
        grid=(nvisit,),
        in_specs=[pl.BlockSpec((bm, D_MODEL), lambda v, blk, ex, lo, hi: (blk[v], 0)),
                  pl.BlockSpec((1, 1, D_MODEL, 2 * D_FF), lambda v, blk, ex, lo, hi: (layer, ex[v], 0, 0)),
                  pl.BlockSpec((1, 1, 1, 2 * D_FF), lambda v, blk, ex, lo, hi: (layer, ex[v], 0, 0)),
                  pl.BlockSpec((1, 1, D_FF, D_MODEL), lambda v, blk, ex, lo, hi: (layer, ex[v], 0, 0)),
                  pl.BlockSpec((1, 1, 1, D_MODEL), lambda v, blk, ex, lo, hi: (layer, ex[v], 0, 0))],
        out_specs=pl.BlockSpec((bm, D_MODEL), lambda v, blk, ex, lo, hi: (blk[v], 0)),
        scratch_shapes=[pltpu.VMEM((D_MODEL, 2 * D_FF), BF16), pltpu.VMEM((D_FF, D_MODEL), BF16)],
    )
    return pl.pallas_call(
        functools.partial(_expert_kernel, bm=bm),
        grid_spec=grid_spec,
        out_shape=jax.ShapeDtypeStruct((A, D_MODEL), BF16),
        compiler_params=_params("arbitrary", vmem=VMEM_LIMIT_EXPERTS),
        name="experts",
    )(*meta, x_sorted, w_up, b_up[:, :, None, :], w_down, b_down[:, :, None, :])


def _combine_kernel(x_ref, y0_ref, y1_ref, y2_ref, y3_ref, g_ref, o_ref):
    acc = x_ref[...]
    g = g_ref[...]
    for k, y_ref in enumerate((y0_ref, y1_ref, y2_ref, y3_ref)):
        acc = acc + y_ref[...].astype(F32) * g[:, k:k + 1]
    o_ref[...] = acc


def _combine(x2, ys, gates):
    T = x2.shape[0]
    tm = TILE
    row = lambda w: pl.BlockSpec((tm, w), lambda i: (i, 0))
    return pl.pallas_call(
        _combine_kernel,
        grid=(T // tm,),
        in_specs=[row(D_MODEL)] * 5 + [row(TOP_K)],
        out_specs=row(D_MODEL),
        out_shape=jax.ShapeDtypeStruct((T, D_MODEL), F32),
        compiler_params=_params("parallel"),
        name="combine",
    )(x2, *ys, gates)


def _visit_schedule(counts, nblk, bm):
    ends = jnp.cumsum(counts)
    starts = ends - counts
    first_blk = starts // bm
    nvis = jnp.where(counts > 0, (ends - 1) // bm - first_blk + 1, 0)
    vis_end = jnp.cumsum(nvis)
    vis_start = vis_end - nvis
    total = vis_end[-1]
    v = jnp.arange(nblk + N_EXPERTS - 1, dtype=jnp.int32)
    e = jnp.sum((vis_end[None, :] <= v[:, None]).astype(jnp.int32), axis=1)
    e = jnp.clip(e, 0, N_EXPERTS - 1)
    valid = v < total
    e_last = jnp.max(jnp.where(counts > 0, jnp.arange(N_EXPERTS, dtype=jnp.int32), 0))
    e = jnp.where(valid, e, e_last)
    blk = jnp.where(valid, first_blk[e] + v - vis_start[e], nblk - 1)
    lo = jnp.where(valid, starts[e], 0)
    hi = jnp.where(valid, ends[e], 0)
    i32 = lambda a: a.astype(jnp.int32)
    return i32(blk), i32(e), i32(lo), i32(hi)


def _moe(x2, h, logits, w_up, b_up, w_down, b_down, layer):
    T = x2.shape[0]
    A = T * TOP_K
    bm = EXPERT_ROWS
    assert A % bm == 0
    top_val, top_idx = lax.top_k(logits, TOP_K)
    gates = jax.nn.softmax(top_val, axis=-1)
    flat_e = top_idx.reshape(-1).astype(jnp.int32)
    iota = jnp.arange(A, dtype=jnp.int32)
    _, order = lax.sort((flat_e, iota), num_keys=1, is_stable=True)
    _, inv = lax.sort((order, iota), num_keys=1)
    counts = jnp.sum((flat_e[:, None] == jnp.arange(N_EXPERTS, dtype=jnp.int32)[None, :]).astype(jnp.int32),
                     axis=0)
    x_sorted = jnp.take(h, order // TOP_K, axis=0, mode='clip')
    meta = _visit_schedule(counts, A // bm, bm)
    y_sorted = _experts(meta, x_sorted, w_up, b_up, w_down, b_down, layer)
    inv4 = inv.reshape(T, TOP_K)
    ys = [jnp.take(y_sorted, inv4[:, k], axis=0, mode='clip') for k in range(TOP_K)]
    return _combine(x2, ys, gates)


def _rotary_tables(seq):
    half = ROT_DIM // 2
    inv_freq = ROPE_THETA ** (-jnp.arange(0, ROT_DIM, 2, dtype=F32) / ROT_DIM)
    ang = jnp.arange(seq).astype(F32)[:, None] * inv_freq[None, :]
    cos, sin = jnp.cos(ang), jnp.sin(ang)
    rest = ATTN_HEAD_DIM - ROT_DIM
    ones = jnp.ones((seq, rest), F32)
    zeros = jnp.zeros((seq, rest), F32)
    zh = jnp.zeros((seq, half), F32)
    cos_a = jnp.concatenate([cos, cos, ones], axis=1)
    sin_b = jnp.concatenate([-sin, zh, zeros], axis=1)
    sin_c = jnp.concatenate([zh, sin, zeros], axis=1)
    tile = lambda a: jnp.tile(a, (1, LANES // ATTN_HEAD_DIM))
    return tile(cos_a), tile(sin_b), tile(sin_c)


def _constants():
    idx = jnp.arange(ATTN_WIDTH) // ATTN_HEAD_DIM
    bd = (idx[:, None] == idx[None, :]).astype(BF16)

    def perm(d):
        new = jnp.arange(TILE)
        old = (new % (TILE // d)) * d + new // (TILE // d)
        return (old[:, None] == jnp.arange(TILE)[None, :]).astype(BF16)

    p4, p16 = perm(4), perm(16)
    expand = (jnp.arange(LANES)[:, None] == idx[None, :]).astype(BF16)
    return dict(bd=bd, p4=p4, p16=p16, p4t=p4.T, p16t=p16.T, expand=expand)


def _layer_weights(l, norm1_g, w_in, q_norm_g, k_norm_g, attn_out_g, conv_w, conv_b, igate_b, fgate_b,
                   mlstm_out_g, w_out, norm2_g, router_w, router_b):
    aw, mw, nh = ATTN_WIDTH, MLSTM_WIDTH, MLSTM_HEADS
    n_main = 3 * aw + 4 * mw
    w = w_in[l]
    wg = w[:, n_main:]
    pick = lambda a, d: jnp.concatenate([a[..., d * nh:(d + 1) * nh],
                                         a[..., (2 + d) * nh:(3 + d) * nh]], axis=-1)
    gate_bias = [jnp.concatenate([igate_b[l, d], fgate_b[l, d]]) for d in range(2)]
    return dict(
        g1=norm1_g[l][None, :], w_main=w[:, :n_main].astype(BF16),
        w_gate=jnp.concatenate([pick(wg, 0), pick(wg, 1)], axis=1),
        qg=jnp.tile(q_norm_g[l], ATTN_HEADS)[None, :] * (ATTN_HEAD_DIM ** -0.5),
        kg=jnp.tile(k_norm_g[l], ATTN_HEADS)[None, :],
        ag=attn_out_g[l][None, :], cw=conv_w[l], cb=conv_b[l][None, :],
        bias_c=[b[None, :] for b in gate_bias], bias_r=[b[:, None] for b in gate_bias],
        og=mlstm_out_g[l][None, :], w_out=w_out[l].astype(BF16), g2=norm2_g[l][None, :],
        rw=router_w[l], rb=router_b[l][None, :])


def kernel(x_prompt, x_sample, norm1_g, w_in, q_norm_g, k_norm_g, attn_out_g, conv_w, conv_b, igate_b,
           fgate_b, mlstm_out_g, w_out, norm2_g, router_w, router_b, w_up, b_up, w_down, b_down):
    assert all(w // (2 * d) == HALO for w, d in SEGMENTS)
    groups = (x_prompt.shape[:2], x_sample.shape[:2])
    span = max(d for _, d in SEGMENTS) * LQ
    assert all(s % span == 0 for _, s in groups), groups
    small = (norm1_g, w_in, q_norm_g, k_norm_g, attn_out_g, conv_w, conv_b, igate_b, fgate_b,
             mlstm_out_g, w_out, norm2_g, router_w, router_b)
    consts = _constants()
    tables = _rotary_tables(max(s for _, s in groups))
    x2 = jnp.concatenate([x_prompt.reshape(-1, D_MODEL), x_sample.reshape(-1, D_MODEL)], axis=0)
    T = x2.shape[0]
    ngate = 2 * MLSTM_HEADS
    chunk_rows = lambda g: jnp.swapaxes(g.reshape(T // CHUNK, CHUNK, ngate), 1, 2)
    for l in range(DEPTH):
        lw = _layer_weights(l, *small)
        (q1, k1, v1, q4, k4, v4, q16, k16, v16, qkm, vm, om, gf, gb) = _inproj(x2, lw, tables, consts, groups)
        segs = [_attn_segment(q1, k1, v1, 1, groups), _attn_segment(q4, k4, v4, 4, groups),
                _attn_segment(q16, k16, v16, 16, groups)]
        hf, qm, km = _mlstm_pass(False, True, (qkm, lw['cw'], lw['cb'], vm, gf, chunk_rows(gf),
                                               lw['bias_c'][0], lw['bias_r'][0]), groups)
        ml = _mlstm_pass(True, False, (qm, km, vm, gb, chunk_rows(gb), lw['bias_c'][1], lw['bias_r'][1],
                                       hf, om, lw['og']), groups)
        x2, h, logits = _outproj([s[0] for s in segs], [s[1] for s in segs], ml, x2, lw, consts)
        x2 = _moe(x2, h, logits, w_up, b_up, w_down, b_down, l)
    t1 = x_prompt.shape[0] * x_prompt.shape[1]
    return (x2[:t1].reshape(x_prompt.shape), x2[t1:].reshape(x_sample.shape))
```

```python
import functools

import jax
import jax.numpy as jnp
from jax import lax
from jax.experimental import pallas as pl
from jax.experimental.pallas import tpu as pltpu

D_MODEL = 1024
DEPTH = 2
ATTN_WIDTH = 512
ATTN_HEAD_DIM = 64
ATTN_HEADS = 8
ROT_DIM = 16
ROPE_THETA = 500000.0
SEGMENTS = ((128, 1), (512, 4), (2048, 16))
HALO = 64
MLSTM_WIDTH = 512
MLSTM_HEAD_DIM = 128
MLSTM_HEADS = 4
CONV_WIDTH = 5
CHUNK = 64
N_EXPERTS = 32
TOP_K = 4
D_FF = 1024
SWIGLU_LIMIT = 7.0
SWIGLU_ALPHA = 1.702
RMS_EPS = 1e-6
NEG_INF = -1e30

LANES = 128
SUBLANES = 8
TILE = 512
LQ = 128
EXPERT_ROWS = 512
VMEM_LIMIT = 48 * 1024 * 1024
VMEM_LIMIT_EXPERTS = 56 * 1024 * 1024

F32 = jnp.float32
BF16 = jnp.bfloat16
NT_DIMS = (((1,), (1,)), ((), ()))
TN_DIMS = (((0,), (0,)), ((), ()))


def _params(*sem, vmem=VMEM_LIMIT):
    return pltpu.CompilerParams(dimension_semantics=sem, vmem_limit_bytes=vmem)


def _seq_bounds(row, groups):
    (b1, s1), (_, s2) = groups
    t1 = b1 * s1
    in1 = row < t1
    start = jnp.where(in1, (row // s1) * s1, t1 + ((row - t1) // s2) * s2)
    return start, start + jnp.where(in1, s1, s2)


def _split3(a):
    hi = a.astype(BF16)
    r1 = a - hi.astype(F32)
    mid = r1.astype(BF16)
    lo = (r1 - mid.astype(F32)).astype(BF16)
    return hi, mid, lo


def _dot_split(a, b, pieces):
    parts = _split3(a)[:pieces]
    out = jnp.dot(parts[0], b, preferred_element_type=F32)
    for p in parts[1:]:
        out = out + jnp.dot(p, b, preferred_element_type=F32)
    return out


def _dot_split_rhs(a, b, pieces):
    parts = _split3(b)[:pieces]
    out = jnp.dot(a, parts[0], preferred_element_type=F32)
    for p in parts[1:]:
        out = out + jnp.dot(a, p, preferred_element_type=F32)
    return out


def _dot_f32(a, b):
    ah, al, _ = _split3(a)
    bh, bl, _ = _split3(b)
    out = jnp.dot(ah, bh, preferred_element_type=F32)
    out = out + jnp.dot(ah, bl, preferred_element_type=F32)
    return out + jnp.dot(al, bh, preferred_element_type=F32)


def _group_sumsq(z, bd_ref):
    return _dot_split(z * z, bd_ref[...], 1)


def _inproj_kernel(x_ref, g1_ref, w_ref, wg_ref, qg_ref, kg_ref, cos_ref, sinb_ref, sinc_ref,
                   bd_ref, p4_ref, p16_ref,
                   q1_ref, k1_ref, v1_ref, q4_ref, k4_ref, v4_ref, q16_ref, k16_ref, v16_ref,
                   qkm_ref, vm_ref, om_ref, gf_ref, gb_ref):
    x = x_ref[...]
    xn = x * lax.rsqrt(jnp.mean(x * x, axis=-1, keepdims=True) + RMS_EPS) * g1_ref[...]
    xb = xn.astype(BF16)
    aw, mw = ATTN_WIDTH, MLSTM_WIDTH

    def proj(lo, hi):
        return jnp.dot(xb, w_ref[:, lo:hi], preferred_element_type=F32)

    reps = aw // LANES
    cos_a = jnp.concatenate([cos_ref[...]] * reps, axis=1)
    sin_b = jnp.concatenate([sinb_ref[...]] * reps, axis=1)
    sin_c = jnp.concatenate([sinc_ref[...]] * reps, axis=1)

    def norm_rot(z, g_ref):
        ss = _group_sumsq(z, bd_ref)
        y = z * lax.rsqrt(ss * (1.0 / ATTN_HEAD_DIM) + RMS_EPS) * g_ref[...]
        half = ROT_DIM // 2
        y_up = pltpu.roll(y, aw - half, axis=1)
        y_dn = pltpu.roll(y, half, axis=1)
        return (y * cos_a + y_up * sin_b + y_dn * sin_c).astype(BF16)

    def emit(z, nat_ref, r4_ref, r16_ref):
        nat_ref[...] = z
        r4_ref[...] = jnp.dot(p4_ref[...], z, preferred_element_type=F32).astype(BF16)
        r16_ref[...] = jnp.dot(p16_ref[...], z, preferred_element_type=F32).astype(BF16)

    emit(norm_rot(proj(0, aw), qg_ref), q1_ref, q4_ref, q16_ref)
    emit(norm_rot(proj(aw, 2 * aw), kg_ref), k1_ref, k4_ref, k16_ref)
    emit(proj(2 * aw, 3 * aw).astype(BF16), v1_ref, v4_ref, v16_ref)
    qkm_ref[...] = proj(3 * aw, 3 * aw + 2 * mw)
    vm_ref[...] = proj(3 * aw + 2 * mw, 3 * aw + 3 * mw).astype(BF16)
    om_ref[...] = proj(3 * aw + 3 * mw, 3 * aw + 4 * mw)
    gates = _dot_f32(xn, wg_ref[...])
    ng = 2 * MLSTM_HEADS
    gf_ref[...] = gates[:, :ng]
    gb_ref[...] = gates[:, ng:]


def _inproj(x2, lw, tables, consts, groups):
    T = x2.shape[0]
    tm = TILE
    aw, mw, ng = ATTN_WIDTH, MLSTM_WIDTH, 2 * MLSTM_HEADS
    row = lambda w: pl.BlockSpec((tm, w), lambda i: (i, 0))
    const = lambda a: pl.BlockSpec(a.shape, lambda i: (0, 0))

    def tab_index(i):
        start, _ = _seq_bounds(i * tm, groups)
        return (i - start // tm, 0)

    tab = pl.BlockSpec((tm, LANES), tab_index)
    weights = (lw['g1'], lw['w_main'], lw['w_gate'], lw['qg'], lw['kg'])
    perms = (consts['bd'], consts['p4'], consts['p16'])
    bf = lambda w: jax.ShapeDtypeStruct((T, w), BF16)
    f32 = lambda w: jax.ShapeDtypeStruct((T, w), F32)
    return pl.pallas_call(
        _inproj_kernel,
        grid=(T // tm,),
        in_specs=[row(D_MODEL)] + [const(a) for a in weights] + [tab, tab, tab] + [const(a) for a in perms],
        out_specs=[row(aw)] * 9 + [row(2 * mw), row(mw), row(mw), row(ng), row(ng)],
        out_shape=[bf(aw)] * 9 + [f32(2 * mw), bf(mw), f32(mw), f32(ng), f32(ng)],
        compiler_params=_params("parallel"),
        name="inproj",
    )(x2, *weights, *tables, *perms)


def _attn_kernel(q_ref, kl_ref, kc_ref, kr_ref, vl_ref, vc_ref, vr_ref, o_ref, lse_ref, *,
                 dilation, groups):
    aw = ATTN_WIDTH
    lq, lk = LQ, LQ + 2 * HALO
    row0 = pl.program_id(0) * (LQ * dilation)
    start, end = _seq_bounds(row0, groups)
    m0 = (row0 - start) // dilation
    sub_len = (end - start) // dilation

    flat = lambda ref, n: ref[...].reshape(n, ref.shape[-1])
    q = flat(q_ref, lq)
    k = jnp.concatenate([flat(kl_ref, HALO), flat(kc_ref, lq), flat(kr_ref, HALO)], axis=0)
    v = jnp.concatenate([flat(vl_ref, HALO), flat(vc_ref, lq), flat(vr_ref, HALO)], axis=0)

    row = lax.broadcasted_iota(jnp.int32, (2 * lq, lk), 0)
    col = lax.broadcasted_iota(jnp.int32, (2 * lq, lk), 1)
    qrow = jnp.where(row >= lq, row - lq, row)
    rel = col - HALO - qrow
    kpos = m0 - HALO + col
    mask = (jnp.abs(rel) <= HALO) & (kpos >= 0) & (kpos < sub_len)

    lane2 = lax.broadcasted_iota(jnp.int32, (2 * lq, LANES), 1)
    row2 = lax.broadcasted_iota(jnp.int32, (2 * lq, LANES), 0)
    own = (lane2 >= ATTN_HEAD_DIM) ^ (row2 < lq)
    lane = lax.broadcasted_iota(jnp.int32, (lq, LANES), 1)
    first = lane < ATTN_HEAD_DIM

    lse_tile =jnp.zeros((lq, LANES), F32)
    outs = []
    for hp in range(aw // LANES):
        sl = slice(hp * LANES, (hp + 1) * LANES)
        qp = q[:, sl]
        q2 = jnp.concatenate([qp, qp], axis=0)
        q2 = jnp.where(own, q2, jnp.zeros_like(q2))
        s = lax.dot_general(q2, k[:, sl], NT_DIMS, preferred_element_type=F32)
        s = jnp.where(mask, s, NEG_INF)
        mx = jnp.max(s, axis=-1, keepdims=True)
        p = jnp.exp(s - mx)
        den = jnp.sum(p, axis=-1, keepdims=True)
        pv = jnp.dot(p.astype(BF16), v[:, sl], preferred_element_type=F32)
        o2 = pv / den
        lse2 = mx + jnp.log(den)
        outs.append(jnp.where(first, o2[:lq], o2[lq:]).astype(BF16))
        lse_tile = jnp.where(lane == 2 * hp, lse2[:lq], lse_tile)
        lse_tile = jnp.where(lane == 2 * hp + 1, lse2[lq:], lse_tile)
    o_ref[...] = jnp.concatenate(outs, axis=1).reshape(o_ref.shape)
    lse_ref[...] = lse_tile.reshape(lse_ref.shape)


def _attn_segment(q, k, v, dilation, groups):
    T, aw = q.shape
    tm = TILE
    per_tile = tm // dilation
    ntile = T // tm
    if per_tile >= LQ:
        nblk = per_tile // LQ
        nhalo = per_tile // HALO
        hb = LQ // HALO
        shape = lambda w: (ntile * dilation, per_tile, w)
        grid = (ntile * nblk, dilation)
        slab = lambda i, r: (i // nblk) * dilation + r

        def cen(w):
            return pl.BlockSpec((1, LQ, w), lambda i, r: (slab(i, r), i % nblk, 0))

        def halo_spec(side):
            def index(i, r):
                j = (i % nblk) * hb + (-1 if side < 0 else hb)
                t = i // nblk + jnp.where(j < 0, -1, 0) + jnp.where(j >= nhalo, 1, 0)
                t = jnp.clip(t, 0, ntile - 1)
                return (t * dilation + r, j % nhalo, 0)
            return pl.BlockSpec((1, HALO, aw), index)
    else:
        tq = LQ // per_tile
        th = HALO // per_tile
        nhb = ntile // th
        shape = lambda w: (ntile, dilation, per_tile, w)
        grid = (ntile // tq, dilation)

        def cen(w):
            return pl.BlockSpec((tq, None, per_tile, w), lambda i, r: (i, r, 0, 0))

        def halo_spec(side):
            def index(i, r):
                j = i * (tq // th) + (-1 if side < 0 else tq // th)
                return (jnp.clip(j, 0, nhb - 1), r, 0, 0)
            return pl.BlockSpec((th, None, per_tile, aw), index)

    left, right = halo_spec(-1), halo_spec(1)
    view = lambda a: a.reshape(shape(a.shape[-1]))
    o, lse = pl.pallas_call(
        functools.partial(_attn_kernel, dilation=dilation, groups=groups),
        grid=grid,
        in_specs=[cen(aw), left, cen(aw), right, left, cen(aw), right],
        out_specs=[cen(aw), cen(LANES)],
        out_shape=[jax.ShapeDtypeStruct(shape(aw), BF16), jax.ShapeDtypeStruct(shape(LANES), F32)],
        compiler_params=_params("parallel", "parallel"),
        name=f"attn_d{dilation}",
    )(view(q), view(k), view(k), view(k), view(v), view(v), view(v))
    return o.reshape(T, aw), lse.reshape(T, LANES)


def _log_sigmoid(x):
    return jnp.minimum(x, 0.0) - jnp.log(1.0 + jnp.exp(-jnp.abs(x)))


def _mlstm_kernel(*refs, tb, reverse, first_pass, groups):
    nh, hd = MLSTM_HEADS, MLSTM_HEAD_DIM
    mw = MLSTM_WIDTH
    if first_pass:
        (qk_ref, qkp_ref, qkn_ref, cw_ref, cb_ref, v_ref, gc_ref, gr_ref, bc_ref, br_ref,
         h_ref, qo_ref, ko_ref, c_scr, m_scr, intra_scr, upd_scr, col_scr, one_scr,
         q_scr, k_scr, xe_scr) = refs
    else:
        (q_ref, k_ref, v_ref, gc_ref, gr_ref, bc_ref, br_ref, hf_ref, om_ref, og_ref,
         h_ref, c_scr, m_scr, intra_scr, upd_scr, col_scr, one_scr) = refs
    step = pl.program_id(0)
    blk = pl.num_programs(0) - 1 - step if reverse else step
    row0 = blk * tb
    start, end = _seq_bounds(row0, groups)
    at_start = row0 == start
    at_end = row0 + tb == end

    @pl.when(at_end if reverse else at_start)
    def _():
        c_scr[...] = jnp.zeros_like(c_scr)
        m_scr[...] = jnp.zeros_like(m_scr)

    if first_pass:
        pad = CONV_WIDTH // 2
        xe_scr[0:SUBLANES, :] = jnp.where(at_start, 0.0, qkp_ref[...])
        xe_scr[SUBLANES:SUBLANES + tb, :] = qk_ref[...]
        xe_scr[SUBLANES + tb:, :] = jnp.where(at_end, 0.0, qkn_ref[...])
        acc = jnp.zeros((tb, 2 * mw), F32) + cb_ref[...]
        for j in range(CONV_WIDTH):
            acc = acc + xe_scr[SUBLANES - pad + j:SUBLANES - pad + j + tb, :] * cw_ref[j:j + 1, :]
        act = acc * jax.nn.sigmoid(acc)
        qb = act[:, :mw].astype(BF16)
        kb = (act[:, mw:] * (hd ** -0.5)).astype(BF16)
        qo_ref[...] = qb
        ko_ref[...] = kb
        q_scr[...] = qb
        k_scr[...] = kb
        q_src, k_src = q_scr, k_scr
    else:
        q_src, k_src = q_ref, k_ref

    ti = lax.broadcasted_iota(jnp.int32, (CHUNK, CHUNK), 0)
    si = lax.broadcasted_iota(jnp.int32, (CHUNK, CHUNK), 1)
    causal = (si >= ti) if reverse else (si <= ti)
    cum_col = causal.astype(BF16)
    cum_row = ((ti >= si) if reverse else (ti <= si)).astype(BF16)
    last = 0 if reverse else CHUNK - 1
    ones_col = (lax.broadcasted_iota(jnp.int32, (CHUNK, hd), 1) == 0).astype(BF16)
    ngate = 2 * nh
    gc_is_i = lax.broadcasted_iota(jnp.int32, (CHUNK, ngate), 1) < nh
    gr_is_i = lax.broadcasted_iota(jnp.int32, (ngate, CHUNK), 0) < nh
    nchunk = tb // CHUNK

    def local_terms(j, carry):
        rows = pl.ds(j * CHUNK, CHUNK)
        a_c = gc_ref[rows, :] + bc_ref[...]
        a_r = gr_ref[j] + br_ref[...]
        gl_c = jnp.where(gc_is_i, a_c, _log_sigmoid(a_c))
        gl_r = jnp.where(gr_is_i, a_r, _log_sigmoid(a_r))
        b_c = _dot_split_rhs(cum_col, gl_c, 3)
        b_r = _dot_split(gl_r, cum_row, 3)
        qc = q_src[rows, :]
        kc = k_src[rows, :]
        vc = v_ref[rows, :]
        for h in range(nh):
            hs = slice(h * hd, (h + 1) * hd)
            bc = b_c[:, nh + h:nh + h + 1]
            br = b_r[nh + h:nh + h + 1, :]
            lir = gl_r[h:h + 1, :]
            lic = gl_c[:, h:h + 1]
            dmat = jnp.where(causal, bc - br + lir, NEG_INF)
            m_loc = jnp.max(dmat, axis=-1, keepdims=True)
            qh, kh, vh = qc[:, hs], kc[:, hs], vc[:, hs]
            s = lax.dot_general(qh, kh, NT_DIMS, preferred_element_type=F32) * jnp.exp(dmat - m_loc)
            v_aug = jnp.concatenate([vh, ones_col], axis=1)
            intra_scr[j, h] = jnp.dot(s.astype(BF16), v_aug, preferred_element_type=F32)
            b_last = bc[last:last + 1, :]
            log_w = b_last - bc + lic
            mw_loc = jnp.max(log_w, axis=0, keepdims=True)
            wv = (jnp.exp(log_w - mw_loc) * v_aug.astype(F32)).astype(BF16)
            upd_scr[j, h] = lax.dot_general(kh, wv, TN_DIMS, preferred_element_type=F32)
            col_scr[j, h, :, 0:1] = bc
            col_scr[j, h, :, 1:2] = m_loc
            one_scr[j, h, :, 0:1] = b_last
            one_scr[j, h, :, 1:2] = mw_loc
        return carry

    for j in range(nchunk):
        local_terms(j, 0)

    def recurrence(j, carry):
        jj = nchunk - 1 - j if reverse else j
        r0 = pl.multiple_of(jj * CHUNK, CHUNK)
        rows = pl.ds(r0, CHUNK)
        qc = q_src[rows, :]
        if not first_pass:
            hfc = hf_ref[rows, :]
            omc = om_ref[rows, :]
        for h in range(nh):
            hs = slice(h * hd, (h + 1) * hd)
            m_prev = m_scr[h:h + 1, 0:1]
            bc = col_scr[jj, h, :, 0:1]
            m_loc = col_scr[jj, h, :, 1:2]
            b_last = one_scr[jj, h, :, 0:1]
            mw_loc = one_scr[jj, h, :, 1:2]
            inter = bc + m_prev
            m_t = jnp.maximum(inter, m_loc)
            c_old = c_scr[h]
            h_aug = (jnp.exp(m_loc - m_t) * intra_scr[jj, h]
                     + jnp.exp(inter - m_t) * jnp.dot(qc[:, hs], c_old.astype(BF16), preferred_element_type=F32))
            num = h_aug[:, :hd]
            den = h_aug[:, hd:hd + 1]
            h_out = num / jnp.maximum(jnp.abs(den), jnp.exp(-m_t))
            m_new = jnp.maximum(b_last + m_prev, mw_loc)
            c_scr[h] = jnp.exp(b_last + m_prev - m_new) * c_old + jnp.exp(mw_loc - m_new) * upd_scr[jj, h]
            m_scr[h:h + 1, :] = jnp.broadcast_to(m_new, (1, LANES))
            if first_pass:
                h_ref[rows, hs] = h_out
            else:
                y = jax.nn.sigmoid(omc[:, hs]) * (hfc[:, hs] + h_out)
                y = y * lax.rsqrt(jnp.mean(y * y, axis=-1, keepdims=True) + RMS_EPS) * og_ref[:, hs]
                h_ref[rows, hs] = y
        return carry

    lax.fori_loop(0, nchunk, recurrence, 0)


def _mlstm_pass(reverse, first_pass, operands, groups):
    mw, nh, hd = MLSTM_WIDTH, MLSTM_HEADS, MLSTM_HEAD_DIM
    tb = TILE
    T = operands[0].shape[0]
    nblk = T // tb
    ngate = 2 * nh
    bidx = (lambda s: nblk - 1 - s) if reverse else (lambda s: s)
    rows = lambda w: pl.BlockSpec((tb, w), lambda s: (bidx(s), 0))
    const = lambda a: pl.BlockSpec(a.shape, lambda s: (0,) * a.ndim)
    gate_rows = pl.BlockSpec((tb // CHUNK, ngate, CHUNK), lambda s: (bidx(s), 0, 0))
    hb = tb // SUBLANES
    nhb = T // SUBLANES
    nchunk = tb // CHUNK
    state = [pltpu.VMEM((nh, hd, 2 * hd), F32), pltpu.VMEM((SUBLANES, LANES), F32),
             pltpu.VMEM((nchunk, nh, CHUNK, 2 * hd), F32), pltpu.VMEM((nchunk, nh, hd, 2 * hd), F32),
             pltpu.VMEM((nchunk, nh, CHUNK, 2), F32), pltpu.VMEM((nchunk, nh, 1, 2), F32)]
    if first_pass:
        qk, cw, cb, v, gc, gr, bc, br = operands
        prev = pl.BlockSpec((SUBLANES, 2 * mw), lambda s: (jnp.maximum(bidx(s) * hb - 1, 0), 0))
        nxt = pl.BlockSpec((SUBLANES, 2 * mw), lambda s: (jnp.minimum((bidx(s) + 1) * hb, nhb - 1), 0))
        in_specs = [rows(2 * mw), prev, nxt, const(cw), const(cb), rows(mw), rows(ngate), gate_rows,
                    const(bc), const(br)]
        args = (qk, qk, qk, cw, cb, v, gc, gr, bc, br)
        out_specs = [rows(mw), rows(mw), rows(mw)]
        out_shape = [jax.ShapeDtypeStruct((T, mw), F32), jax.ShapeDtypeStruct((T, mw), BF16),
                     jax.ShapeDtypeStruct((T, mw), BF16)]
        scratch = state + [pltpu.VMEM((tb, mw), BF16), pltpu.VMEM((tb, mw), BF16),
                           pltpu.VMEM((tb + 2 * SUBLANES, 2 * mw), F32)]
    else:
        q, k, v, gc, gr, bc, br, hf, om, og = operands
        in_specs = [rows(mw), rows(mw), rows(mw), rows(ngate), gate_rows, const(bc), const(br),
                    rows(mw), rows(mw), const(og)]
        args = operands
        out_specs = rows(mw)
        out_shape = jax.ShapeDtypeStruct((T, mw), F32)
        scratch = state
    return pl.pallas_call(
        functools.partial(_mlstm_kernel, tb=tb, reverse=reverse, first_pass=first_pass, groups=groups),
        grid=(nblk,),
        in_specs=in_specs,
        out_specs=out_specs,
        out_shape=out_shape,
        scratch_shapes=scratch,
        compiler_params=_params("arbitrary"),
        name="mlstm_fwd" if first_pass else "mlstm_bwd",
    )(*args)


def _outproj_kernel(o1_ref, o4_ref, o16_ref, l1_ref, l4_ref, l16_ref, ml_ref, x_ref, ag_ref, bd_ref,
                    p4t_ref, p16t_ref, ex_ref, w_ref, g2_ref, rw_ref, rb_ref, y_ref, h_ref, lg_ref):
    unperm = lambda p_ref, o_ref: jnp.dot(p_ref[...], o_ref[...], preferred_element_type=F32)
    o1 = o1_ref[...].astype(F32)
    o4 = unperm(p4t_ref, o4_ref)
    o16 = unperm(p16t_ref, o16_ref)

    def unperm_f32(p_ref, l_ref):
        return _dot_split_rhs(p_ref[...], l_ref[...], 2)

    l1 = l1_ref[...]
    l4 = unperm_f32(p4t_ref, l4_ref)
    l16 = unperm_f32(p16t_ref, l16_ref)
    mx = jnp.maximum(jnp.maximum(l1, l4), l16)
    e1, e4, e16 = jnp.exp(l1 - mx), jnp.exp(l4 - mx), jnp.exp(l16 - mx)
    inv = 1.0 / (e1 + e4 + e16)
    expand = lambda w: _dot_split(w, ex_ref[...], 2)
    attn = o1 + expand(e4 * inv) * (o4 - o1) + expand(e16 * inv) * (o16 - o1)
    ss = _group_sumsq(attn, bd_ref)
    attn = attn * lax.rsqrt(ss * (1.0 / ATTN_HEAD_DIM) + RMS_EPS) * ag_ref[...]
    aw = ATTN_WIDTH
    y = jnp.dot(attn.astype(BF16), w_ref[:aw, :], preferred_element_type=F32)
    y = y + jnp.dot(ml_ref[...].astype(BF16), w_ref[aw:, :], preferred_element_type=F32)
    x = x_ref[...] + y
    y_ref[...] = x
    h = x * lax.rsqrt(jnp.mean(x * x, axis=-1, keepdims=True) + RMS_EPS) * g2_ref[...]
    h_ref[...] = h.astype(BF16)
    lg_ref[...] = _dot_f32(h, rw_ref[...]) + rb_ref[...]


def _outproj(os_, ls_, ml, x2, lw, consts):
    T = x2.shape[0]
    tm = TILE
    row = lambda w: pl.BlockSpec((tm, w), lambda i: (i, 0))
    const = lambda a: pl.BlockSpec(a.shape, lambda i: (0, 0))
    aw = ATTN_WIDTH
    cs = (lw['ag'], consts['bd'], consts['p4t'], consts['p16t'], consts['expand'], lw['w_out'],
          lw['g2'], lw['rw'], lw['rb'])
    return pl.pallas_call(
        _outproj_kernel,
        grid=(T // tm,),
        in_specs=[row(aw)] * 3 + [row(LANES)] * 3 + [row(MLSTM_WIDTH), row(D_MODEL)] + [const(a) for a in cs],
        out_specs=[row(D_MODEL), row(D_MODEL), row(N_EXPERTS)],
        out_shape=[jax.ShapeDtypeStruct((T, D_MODEL), F32), jax.ShapeDtypeStruct((T, D_MODEL), BF16),
                   jax.ShapeDtypeStruct((T, N_EXPERTS), F32)],
        compiler_params=_params("parallel"),
        name="outproj",
    )(*os_, *ls_, ml, x2, *cs)


def _expert_kernel(blk_ref, exp_ref, lo_ref, hi_ref, x_ref, wu_ref, bu_ref, wd_ref, bd_ref, y_ref,
                   wu_scr, wd_scr, *, bm):
    v = pl.program_id(0)
    pv = jnp.maximum(v - 1, 0)
    new_expert = (v == 0) | (exp_ref[v] != exp_ref[pv])
    first_visit = (v == 0) | (blk_ref[v] != blk_ref[pv])
    cast_rows = 128

    @pl.when(new_expert)
    def _():
        def cast(i, c):
            r = pl.ds(pl.multiple_of(i * cast_rows, cast_rows), cast_rows)
            wu_scr[r, :] = wu_ref[0, 0, r, :].astype(BF16)
            wd_scr[r, :] = wd_ref[0, 0, r, :].astype(BF16)
            return c
        lax.fori_loop(0, D_MODEL // cast_rows, cast, 0)

    lo, hi = lo_ref[v], hi_ref[v]

    @pl.when(hi > lo)
    def _():
        x = x_ref[...]
        half = D_FF // 2
        y = jnp.zeros((bm, D_MODEL), F32) + bd_ref[0, 0]
        for c in range(2):
            g0, l0 = c * half, D_FF + c * half
            x_glu = jnp.dot(x, wu_scr[:, g0:g0 + half], preferred_element_type=F32) + bu_ref[0, 0, :, g0:g0 + half]
            x_lin = jnp.dot(x, wu_scr[:, l0:l0 + half], preferred_element_type=F32) + bu_ref[0, 0, :, l0:l0 + half]
            x_glu = jnp.minimum(x_glu, SWIGLU_LIMIT)
            x_lin = jnp.clip(x_lin, -SWIGLU_LIMIT, SWIGLU_LIMIT)
            act = x_glu * jax.nn.sigmoid(SWIGLU_ALPHA * x_glu) * (x_lin + 1.0)
            y = y + jnp.dot(act.astype(BF16), wd_scr[g0:g0 + half, :], preferred_element_type=F32)
        rows = blk_ref[v] * bm + lax.broadcasted_iota(jnp.int32, (bm, 1), 0)
        mine = (rows >= lo) & (rows < hi)

        @pl.when(first_visit)
        def _():
            y_ref[...] = jnp.where(mine, y, 0.0).astype(y_ref.dtype)

        @pl.when(jnp.logical_not(first_visit))
        def _():
            y_ref[...] = jnp.where(mine, y.astype(y_ref.dtype), y_ref[...])


def _experts(meta, x_sorted, w_up, b_up, w_down, b_down, layer):
    A = x_sorted.shape[0]
    bm = EXPERT_ROWS
    nvisit = meta[0].shape[0]
    grid_spec = pltpu.PrefetchScalarGridSpec(
        num_scalar_prefetch=4,
        grid=(nvisit,),
        in_specs=[pl.BlockSpec((bm, D_MODEL), lambda v, blk, ex, lo, hi: (blk[v], 0)),
                  pl.BlockSpec((1, 1, D_MODEL, 2 * D_FF), lambda v, blk, ex, lo, hi: (layer, ex[v], 0, 0)),
                  pl.BlockSpec((1, 1, 1, 2 * D_FF), lambda v, blk, ex, lo, hi: (layer, ex[v], 0, 0)),
                  pl.BlockSpec((1, 1, D_FF, D_MODEL), lambda v, blk, ex, lo, hi: (layer, ex[v], 0, 0)),
                  pl.BlockSpec((1, 1, 1, D_MODEL), lambda v, blk, ex, lo, hi: (layer, ex[v], 0, 0))],
        out_specs=pl.BlockSpec((bm, D_MODEL), lambda v, blk, ex, lo, hi: (blk[v], 0)),
        scratch_shapes=[pltpu.VMEM((D_MODEL, 2 * D_FF), BF16), pltpu.VMEM((D_FF, D_MODEL), BF16)],
    )
    return pl.pallas_call(
        functools.partial(_expert_kernel, bm=bm),
        grid_spec=grid_spec,
        out_shape=jax.ShapeDtypeStruct((A, D_MODEL), BF16),
        compiler_params=_params("arbitrary", vmem=VMEM_LIMIT_EXPERTS),
        name="experts",
    )(*meta, x_sorted, w_up, b_up[:, :, None, :], w_down, b_down[:, :, None, :])


def _combine_kernel(x_ref, y0_ref, y1_ref, y2_ref, y3_ref, g_ref, o_ref):
    acc = x_ref[...]
    g = g_ref[...]
    for k, y_ref in enumerate((y0_ref, y1_ref, y2_ref, y3_ref)):
        acc = acc + y_ref[...].astype(F32) * g[:, k:k + 1]
    o_ref[...] = acc


def _combine(x2, ys, gates):
    T = x2.shape[0]
    tm = TILE
    row = lambda w: pl.BlockSpec((tm, w), lambda i: (i, 0))
    return pl.pallas_call(
        _combine_kernel,
        grid=(T // tm,),
        in_specs=[row(D_MODEL)] * 5 + [row(TOP_K)],
        out_specs=row(D_MODEL),
        out_shape=jax.ShapeDtypeStruct((T, D_MODEL), F32),
        compiler_params=_params("parallel"),
        name="combine",
    )(x2, *ys, gates)


def _visit_schedule(counts, nblk, bm):
    ends = jnp.cumsum(counts)
    starts = ends - counts
    first_blk = starts // bm
    nvis = jnp.where(counts > 0, (ends - 1) // bm - first_blk + 1, 0)
    vis_end = jnp.cumsum(nvis)
    vis_start = vis_end - nvis
    total = vis_end[-1]
    v = jnp.arange(nblk + N_EXPERTS - 1, dtype=jnp.int32)
    e = jnp.sum((vis_end[None, :] <= v[:, None]).astype(jnp.int32), axis=1)
    e = jnp.clip(e, 0, N_EXPERTS - 1)
    valid = v < total
    e_last = jnp.max(jnp.where(counts > 0, jnp.arange(N_EXPERTS, dtype=jnp.int32), 0))
    e = jnp.where(valid, e, e_last)
    blk = jnp.where(valid, first_blk[e] + v - vis_start[e], nblk - 1)
    lo = jnp.where(valid, starts[e], 0)
    hi = jnp.where(valid, ends[e], 0)
    i32 = lambda a: a.astype(jnp.int32)
    return i32(blk), i32(e), i32(lo), i32(hi)


def _moe(x2, h, logits, w_up, b_up, w_down, b_down, layer):
    T = x2.shape[0]
    A = T * TOP_K
    bm = EXPERT_ROWS
    assert A % bm == 0
    top_val, top_idx = lax.top_k(logits, TOP_K)
    gates = jax.nn.softmax(top_val, axis=-1)
    flat_e = top_idx.reshape(-1).astype(jnp.int32)
    iota = jnp.arange(A, dtype=jnp.int32)
    _, order = lax.sort((flat_e, iota), num_keys=1, is_stable=True)
    _, inv = lax.sort((order, iota), num_keys=1)
    counts = jnp.sum((flat_e[:, None] == jnp.arange(N_EXPERTS, dtype=jnp.int32)[None, :]).astype(jnp.int32),
                     axis=0)
    x_sorted = jnp.take(h, order // TOP_K, axis=0, mode='clip')
    meta = _visit_schedule(counts, A // bm, bm)
    y_sorted = _experts(meta, x_sorted, w_up, b_up, w_down, b_down, layer)
    inv4 = inv.reshape(T, TOP_K)
    ys = [jnp.take(y_sorted, inv4[:, k], axis=0, mode='clip') for k in range(TOP_K)]
    return _combine(x2, ys, gates)


def _rotary_tables(seq):
    half = ROT_DIM // 2
    inv_freq = ROPE_THETA ** (-jnp.arange(0, ROT_DIM, 2, dtype=F32) / ROT_DIM)
    ang = jnp.arange(seq).astype(F32)[:, None] * inv_freq[None, :]
    cos, sin = jnp.cos(ang), jnp.sin(ang)
    rest = ATTN_HEAD_DIM - ROT_DIM
    ones = jnp.ones((seq, rest), F32)
    zeros = jnp.zeros((seq, rest), F32)
    zh = jnp.zeros((seq, half), F32)
    cos_a = jnp.concatenate([cos, cos, ones], axis=1)
    sin_b = jnp.concatenate([-sin, zh, zeros], axis=1)
    sin_c = jnp.concatenate([zh, sin, zeros], axis=1)
    tile = lambda a: jnp.tile(a, (1, LANES // ATTN_HEAD_DIM))
    return tile(cos_a), tile(sin_b), tile(sin_c)


def _constants():
    idx = jnp.arange(ATTN_WIDTH) // ATTN_HEAD_DIM
    bd = (idx[:, None] == idx[None, :]).astype(BF16)

    def perm(d):
        new = jnp.arange(TILE)
        old = (new % (TILE // d)) * d + new // (TILE // d)
        return (old[:, None] == jnp.arange(TILE)[None, :]).astype(BF16)

    p4, p16 = perm(4), perm(16)
    expand = (jnp.arange(LANES)[:, None] == idx[None, :]).astype(BF16)
    return dict(bd=bd, p4=p4, p16=p16, p4t=p4.T, p16t=p16.T, expand=expand)


def _layer_weights(l, norm1_g, w_in, q_norm_g, k_norm_g, attn_out_g, conv_w, conv_b, igate_b, fgate_b,
                   mlstm_out_g, w_out, norm2_g, router_w, router_b):
    aw, mw, nh = ATTN_WIDTH, MLSTM_WIDTH, MLSTM_HEADS
    n_main = 3 * aw + 4 * mw
    w = w_in[l]
    wg = w[:, n_main:]
    pick = lambda a, d: jnp.concatenate([a[..., d * nh:(d + 1) * nh],
                                         a[..., (2 + d) * nh:(3 + d) * nh]], axis=-1)
    gate_bias = [jnp.concatenate([igate_b[l, d], fgate_b[l, d]]) for d in range(2)]
    return dict(
        g1=norm1_g[l][None, :], w_main=w[:, :n_main].astype(BF16),
        w_gate=jnp.concatenate([pick(wg, 0), pick(wg, 1)], axis=1),
        qg=jnp.tile(q_norm_g[l], ATTN_HEADS)[None, :] * (ATTN_HEAD_DIM ** -0.5),
        kg=jnp.tile(k_norm_g[l], ATTN_HEADS)[None, :],
        ag=attn_out_g[l][None, :], cw=conv_w[l], cb=conv_b[l][None, :],
        bias_c=[b[None, :] for b in gate_bias], bias_r=[b[:, None] for b in gate_bias],
        og=mlstm_out_g[l][None, :], w_out=w_out[l].astype(BF16), g2=norm2_g[l][None, :],
        rw=router_w[l], rb=router_b[l][None, :])


def kernel(x_prompt, x_sample, norm1_g, w_in, q_norm_g, k_norm_g, attn_out_g, conv_w, conv_b, igate_b,
           fgate_b, mlstm_out_g, w_out, norm2_g, router_w, router_b, w_up, b_up, w_down, b_down):
    assert all(w // (2 * d) == HALO for w, d in SEGMENTS)
    groups = (x_prompt.shape[:2], x_sample.shape[:2])
    span = max(d for _, d in SEGMENTS) * LQ
    assert all(s % span == 0 for _, s in groups), groups
    small = (norm1_g, w_in, q_norm_g, k_norm_g, attn_out_g, conv_w, conv_b, igate_b, fgate_b,
             mlstm_out_g, w_out, norm2_g, router_w, router_b)
    consts = _constants()
    tables = _rotary_tables(max(s for _, s in groups))
    ngate = 2 * MLSTM_HEADS

    def layer(x2, lw, l, grp):
        chunk_rows = lambda g: jnp.swapaxes(g.reshape(x2.shape[0] // CHUNK, CHUNK, ngate), 1, 2)
        (q1, k1, v1, q4, k4, v4, q16, k16, v16, qkm, vm, om, gf, gb) = _inproj(x2, lw, tables, consts, grp)
        segs = [_attn_segment(q1, k1, v1, 1, grp), _attn_segment(q4, k4, v4, 4, grp),
                _attn_segment(q16, k16, v16, 16, grp)]
        hf, qm, km = _mlstm_pass(False, True, (qkm, lw['cw'], lw['cb'], vm, gf, chunk_rows(gf),
                                               lw['bias_c'][0], lw['bias_r'][0]), grp)
        ml = _mlstm_pass(True, False, (qm, km, vm, gb, chunk_rows(gb), lw['bias_c'][1], lw['bias_r'][1],
                                       hf, om, lw['og']), grp)
        x2, h, logits = _outproj([s[0] for s in segs], [s[1] for s in segs], ml, x2, lw, consts)
        return _moe(x2, h, logits, w_up, b_up, w_down, b_down, l)

    (b1, s1), (b2, s2) = groups
    t1 = b1 * s1
    n_a = min(max(round(((t1 + b2 * s2) / 2 - t1) / s2), 0), b2)
    xp = x_prompt.reshape(-1, D_MODEL)
    xs = x_sample.reshape(-1, D_MODEL)
    streams = [(jnp.concatenate([xp, xs[:n_a * s2]], axis=0), ((b1, s1), (n_a, s2)))]
    if n_a < b2:
        streams.append((xs[n_a * s2:], ((b2 - n_a, s2), (0, s2))))
    for l in range(DEPTH):
        lw = _layer_weights(l, *small)
        streams = [(layer(x2, lw, l, grp), grp) for x2, grp in streams]
    outs = [x2 for x2, _ in streams]
    y_sample = jnp.concatenate([outs[0][t1:]] + outs[1:], axis=0)
    return (outs[0][:t1].reshape(x_prompt.shape), y_sample.reshape(x_sample.shape))
```

```python
import functools

import jax
import jax.numpy as jnp
from jax import lax
from jax.experimental import pallas as pl
from jax.experimental.pallas import tpu as pltpu

D_MODEL = 1024
DEPTH = 2
ATTN_WIDTH = 512
ATTN_HEAD_DIM = 64
ATTN_HEADS = 8
ROT_DIM = 16
ROPE_THETA = 500000.0
SEGMENTS = ((128, 1), (512, 4), (2048, 16))
HALO = 64
MLSTM_WIDTH = 512
MLSTM_HEAD_DIM = 128
MLSTM_HEADS = 4
CONV_WIDTH = 5
CHUNK = 64
N_EXPERTS = 32
TOP_K = 4
D_FF = 1024
SWIGLU_LIMIT = 7.0
SWIGLU_ALPHA = 1.702
RMS_EPS = 1e-6
NEG_INF = -1e30

LANES = 128
SUBLANES = 8
TILE = 512
LQ = 128
EXPERT_ROWS = 512
VMEM_LIMIT = 48 * 1024 * 1024
VMEM_LIMIT_EXPERTS = 56 * 1024 * 1024

F32 = jnp.float32
BF16 = jnp.bfloat16
NT_DIMS = (((1,), (1,)), ((), ()))
TN_DIMS = (((0,), (0,)), ((), ()))


def _params(*sem, vmem=VMEM_LIMIT):
    return pltpu.CompilerParams(dimension_semantics=sem, vmem_limit_bytes=vmem)


def _seq_bounds(row, groups):
    (b1, s1), (_, s2) = groups
    t1 = b1 * s1
    in1 = row < t1
    start = jnp.where(in1, (row // s1) * s1, t1 + ((row - t1) // s2) * s2)
    return start, start + jnp.where(in1, s1, s2)


def _split3(a):
    hi = a.astype(BF16)
    r1 = a - hi.astype(F32)
    mid = r1.astype(BF16)
    lo = (r1 - mid.astype(F32)).astype(BF16)
    return hi, mid, lo


def _dot_split(a, b, pieces):
    parts = _split3(a)[:pieces]
    out = jnp.dot(parts[0], b, preferred_element_type=F32)
    for p in parts[1:]:
        out = out + jnp.dot(p, b, preferred_element_type=F32)
    return out


def _dot_split_rhs(a, b, pieces):
    parts = _split3(b)[:pieces]
    out = jnp.dot(a, parts[0], preferred_element_type=F32)
    for p in parts[1:]:
        out = out + jnp.dot(a, p, preferred_element_type=F32)
    return out


def _dot_f32(a, b):
    ah, al, _ = _split3(a)
    bh, bl, _ = _split3(b)
    out = jnp.dot(ah, bh, preferred_element_type=F32)
    out = out + jnp.dot(ah, bl, preferred_element_type=F32)
    return out + jnp.dot(al, bh, preferred_element_type=F32)


def _group_sumsq(z, bd_ref):
    return _dot_split(z * z, bd_ref[...], 1)


def _inproj_kernel(x_ref, g1_ref, w_ref, wg_ref, qg_ref, kg_ref, cos_ref, sinb_ref, sinc_ref,
                   bd_ref, p4_ref, p16_ref,
                   q1_ref, k1_ref, v1_ref, q4_ref, k4_ref, v4_ref, q16_ref, k16_ref, v16_ref,
                   qkm_ref, vm_ref, om_ref, gt_ref):
    x = x_ref[...]
    xn = x * lax.rsqrt(jnp.mean(x * x, axis=-1, keepdims=True) + RMS_EPS) * g1_ref[...]
    xb = xn.astype(BF16)
    aw, mw = ATTN_WIDTH, MLSTM_WIDTH

    def proj(lo, hi):
        return jnp.dot(xb, w_ref[:, lo:hi], preferred_element_type=F32)

    reps = aw // LANES
    cos_a = jnp.concatenate([cos_ref[...]] * reps, axis=1)
    sin_b = jnp.concatenate([sinb_ref[...]] * reps, axis=1)
    sin_c = jnp.concatenate([sinc_ref[...]] * reps, axis=1)

    def norm_rot(z, g_ref):
        ss = _group_sumsq(z, bd_ref)
        y = z * lax.rsqrt(ss * (1.0 / ATTN_HEAD_DIM) + RMS_EPS) * g_ref[...]
        half = ROT_DIM // 2
        y_up = pltpu.roll(y, aw - half, axis=1)
        y_dn = pltpu.roll(y, half, axis=1)
        return (y * cos_a + y_up * sin_b + y_dn * sin_c).astype(BF16)

    def emit(z, nat_ref, r4_ref, r16_ref):
        nat_ref[...] = z
        r4_ref[...] = jnp.dot(p4_ref[...], z, preferred_element_type=F32).astype(BF16)
        r16_ref[...] = jnp.dot(p16_ref[...], z, preferred_element_type=F32).astype(BF16)

    emit(norm_rot(proj(0, aw), qg_ref), q1_ref, q4_ref, q16_ref)
    emit(norm_rot(proj(aw, 2 * aw), kg_ref), k1_ref, k4_ref, k16_ref)
    emit(proj(2 * aw, 3 * aw).astype(BF16), v1_ref, v4_ref, v16_ref)
    qkm_ref[...] = proj(3 * aw, 3 * aw + 2 * mw)
    vm_ref[...] = proj(3 * aw + 2 * mw, 3 * aw + 3 * mw).astype(BF16)
    om_ref[...] = proj(3 * aw + 3 * mw, 3 * aw + 4 * mw)
    gt_ref[...] = _dot_f32(xn, wg_ref[...])


def _inproj(x2, lw, tables, consts, groups):
    T = x2.shape[0]
    tm = TILE
    aw, mw = ATTN_WIDTH, MLSTM_WIDTH
    row = lambda w: pl.BlockSpec((tm, w), lambda i: (i, 0))
    const = lambda a: pl.BlockSpec(a.shape, lambda i: (0, 0))

    def tab_index(i):
        start, _ = _seq_bounds(i * tm, groups)
        return (i - start // tm, 0)

    tab = pl.BlockSpec((tm, LANES), tab_index)
    weights = (lw['g1'], lw['w_main'], lw['w_gate'], lw['qg'], lw['kg'])
    perms = (consts['bd'], consts['p4'], consts['p16'])
    bf = lambda w: jax.ShapeDtypeStruct((T, w), BF16)
    f32 = lambda w: jax.ShapeDtypeStruct((T, w), F32)
    return pl.pallas_call(
        _inproj_kernel,
        grid=(T // tm,),
        in_specs=[row(D_MODEL)] + [const(a) for a in weights] + [tab, tab, tab] + [const(a) for a in perms],
        out_specs=[row(aw)] * 9 + [row(2 * mw), row(mw), row(mw), row(LANES)],
        out_shape=[bf(aw)] * 9 + [f32(2 * mw), bf(mw), f32(mw), f32(LANES)],
        compiler_params=_params("parallel"),
        name="inproj",
    )(x2, *weights, *tables, *perms)


def _attn_kernel(q_ref, kl_ref, kc_ref, kr_ref, vl_ref, vc_ref, vr_ref, o_ref, lse_ref, *,
                 dilation, groups):
    aw = ATTN_WIDTH
    lq, lk = LQ, LQ + 2 * HALO
    row0 = pl.program_id(0) * (LQ * dilation)
    start, end = _seq_bounds(row0, groups)
    m0 = (row0 - start) // dilation
    sub_len = (end - start) // dilation

    flat = lambda ref, n: ref[...].reshape(n, ref.shape[-1])
    q = flat(q_ref, lq)
    k = jnp.concatenate([flat(kl_ref, HALO), flat(kc_ref, lq), flat(kr_ref, HALO)], axis=0)
    v = jnp.concatenate([flat(vl_ref, HALO), flat(vc_ref, lq), flat(vr_ref, HALO)], axis=0)

    row = lax.broadcasted_iota(jnp.int32, (2 * lq, lk), 0)
    col = lax.broadcasted_iota(jnp.int32, (2 * lq, lk), 1)
    qrow = jnp.where(row >= lq, row - lq, row)
    rel = col - HALO - qrow
    kpos = m0 - HALO + col
    mask = (jnp.abs(rel) <= HALO) & (kpos >= 0) & (kpos < sub_len)

    lane2 = lax.broadcasted_iota(jnp.int32, (2 * lq, LANES), 1)
    row2 = lax.broadcasted_iota(jnp.int32, (2 * lq, LANES), 0)
    own = (lane2 >= ATTN_HEAD_DIM) ^ (row2 < lq)
    lane = lax.broadcasted_iota(jnp.int32, (lq, LANES), 1)
    first = lane < ATTN_HEAD_DIM

    lse_tile =jnp.zeros((lq, LANES), F32)
    outs = []
    for hp in range(aw // LANES):
        sl = slice(hp * LANES, (hp + 1) * LANES)
        qp = q[:, sl]
        q2 = jnp.concatenate([qp, qp], axis=0)
        q2 = jnp.where(own, q2, jnp.zeros_like(q2))
        s = lax.dot_general(q2, k[:, sl], NT_DIMS, preferred_element_type=F32)
        s = jnp.where(mask, s, NEG_INF)
        mx = jnp.max(s, axis=-1, keepdims=True)
        p = jnp.exp(s - mx)
        den = jnp.sum(p, axis=-1, keepdims=True)
        pv = jnp.dot(p.astype(BF16), v[:, sl], preferred_element_type=F32)
        o2 = pv / den
        lse2 = mx + jnp.log(den)
        outs.append(jnp.where(first, o2[:lq], o2[lq:]).astype(BF16))
        lse_tile = jnp.where(lane == 2 * hp, lse2[:lq], lse_tile)
        lse_tile = jnp.where(lane == 2 * hp + 1, lse2[lq:], lse_tile)
    o_ref[...] = jnp.concatenate(outs, axis=1).reshape(o_ref.shape)
    lse_ref[...] = lse_tile.reshape(lse_ref.shape)


def _attn_segment(q, k, v, dilation, groups):
    T, aw = q.shape
    tm = TILE
    per_tile = tm // dilation
    ntile = T // tm
    if per_tile >= LQ:
        nblk = per_tile // LQ
        nhalo = per_tile // HALO
        hb = LQ // HALO
        shape = lambda w: (ntile * dilation, per_tile, w)
        grid = (ntile * nblk, dilation)
        slab = lambda i, r: (i // nblk) * dilation + r

        def cen(w):
            return pl.BlockSpec((1, LQ, w), lambda i, r: (slab(i, r), i % nblk, 0))

        def halo_spec(side):
            def index(i, r):
                j = (i % nblk) * hb + (-1 if side < 0 else hb)
                t = i // nblk + jnp.where(j < 0, -1, 0) + jnp.where(j >= nhalo, 1, 0)
                t = jnp.clip(t, 0, ntile - 1)
                return (t * dilation + r, j % nhalo, 0)
            return pl.BlockSpec((1, HALO, aw), index)
    else:
        tq = LQ // per_tile
        th = HALO // per_tile
        nhb = ntile // th
        shape = lambda w: (ntile, dilation, per_tile, w)
        grid = (ntile // tq, dilation)

        def cen(w):
            return pl.BlockSpec((tq, None, per_tile, w), lambda i, r: (i, r, 0, 0))

        def halo_spec(side):
            def index(i, r):
                j = i * (tq // th) + (-1 if side < 0 else tq // th)
                return (jnp.clip(j, 0, nhb - 1), r, 0, 0)
            return pl.BlockSpec((th, None, per_tile, aw), index)

    left, right = halo_spec(-1), halo_spec(1)
    view = lambda a: a.reshape(shape(a.shape[-1]))
    o, lse = pl.pallas_call(
        functools.partial(_attn_kernel, dilation=dilation, groups=groups),
        grid=grid,
        in_specs=[cen(aw), left, cen(aw), right, left, cen(aw), right],
        out_specs=[cen(aw), cen(LANES)],
        out_shape=[jax.ShapeDtypeStruct(shape(aw), BF16), jax.ShapeDtypeStruct(shape(LANES), F32)],
        compiler_params=_params("parallel", "parallel"),
        name=f"attn_d{dilation}",
    )(view(q), view(k), view(k), view(k), view(v), view(v), view(v))
    return o.reshape(T, aw), lse.reshape(T, LANES)


def _log_sigmoid(x):
    return jnp.minimum(x, 0.0) - jnp.log(1.0 + jnp.exp(-jnp.abs(x)))


def _mlstm_kernel(*refs, tb, reverse, first_pass, groups):
    nh, hd = MLSTM_HEADS, MLSTM_HEAD_DIM
    goff = 2 * nh if reverse else 0
    mw = MLSTM_WIDTH
    if first_pass:
        (qk_ref, qkp_ref, qkn_ref, cw_ref, cb_ref, v_ref, gc_ref, gr_ref, bc_ref, br_ref,
         h_ref, qo_ref, ko_ref, c_scr, m_scr, intra_scr, upd_scr, col_scr, one_scr,
         q_scr, k_scr, xe_scr) = refs
    else:
        (q_ref, k_ref, v_ref, gc_ref, gr_ref, bc_ref, br_ref, hf_ref, om_ref, og_ref,
         h_ref, c_scr, m_scr, intra_scr, upd_scr, col_scr, one_scr) = refs
    step = pl.program_id(0)
    blk = pl.num_programs(0) - 1 - step if reverse else step
    row0 = blk * tb
    start, end = _seq_bounds(row0, groups)
    at_start = row0 == start
    at_end = row0 + tb == end

    @pl.when(at_end if reverse else at_start)
    def _():
        c_scr[...] = jnp.zeros_like(c_scr)
        m_scr[...] = jnp.zeros_like(m_scr)

    if first_pass:
        pad = CONV_WIDTH // 2
        xe_scr[0:SUBLANES, :] = jnp.where(at_start, 0.0, qkp_ref[...])
        xe_scr[SUBLANES:SUBLANES + tb, :] = qk_ref[...]
        xe_scr[SUBLANES + tb:, :] = jnp.where(at_end, 0.0, qkn_ref[...])
        acc = jnp.zeros((tb, 2 * mw), F32) + cb_ref[...]
        for j in range(CONV_WIDTH):
            acc = acc + xe_scr[SUBLANES - pad + j:SUBLANES - pad + j + tb, :] * cw_ref[j:j + 1, :]
        act = acc * jax.nn.sigmoid(acc)
        qb = act[:, :mw].astype(BF16)
        kb = (act[:, mw:] * (hd ** -0.5)).astype(BF16)
        qo_ref[...] = qb
        ko_ref[...] = kb
        q_scr[...] = qb
        k_scr[...] = kb
        q_src, k_src = q_scr, k_scr
    else:
        q_src, k_src = q_ref, k_ref

    ti = lax.broadcasted_iota(jnp.int32, (CHUNK, CHUNK), 0)
    si = lax.broadcasted_iota(jnp.int32, (CHUNK, CHUNK), 1)
    causal = (si >= ti) if reverse else (si <= ti)
    cum_col = causal.astype(BF16)
    cum_row = ((ti >= si) if reverse else (ti <= si)).astype(BF16)
    last = 0 if reverse else CHUNK - 1
    ones_col = (lax.broadcasted_iota(jnp.int32, (CHUNK, hd), 1) == 0).astype(BF16)
    ngate = 2 * nh
    gc_is_i = lax.broadcasted_iota(jnp.int32, (CHUNK, ngate), 1) < nh
    gr_is_i = lax.broadcasted_iota(jnp.int32, (ngate, CHUNK), 0) < nh
    nchunk = tb // CHUNK

    def local_terms(j, carry):
        rows = pl.ds(j * CHUNK, CHUNK)
        a_c = gc_ref[rows, goff:goff + ngate] + bc_ref[...]
        a_r = gr_ref[j] + br_ref[...]
        gl_c = jnp.where(gc_is_i, a_c, _log_sigmoid(a_c))
        gl_r = jnp.where(gr_is_i, a_r, _log_sigmoid(a_r))
        b_c = _dot_split_rhs(cum_col, gl_c, 3)
        b_r = _dot_split(gl_r, cum_row, 3)
        qc = q_src[rows, :]
        kc = k_src[rows, :]
        vc = v_ref[rows, :]
        for h in range(nh):
            hs = slice(h * hd, (h + 1) * hd)
            bc = b_c[:, nh + h:nh + h + 1]
            br = b_r[nh + h:nh + h + 1, :]
            lir = gl_r[h:h + 1, :]
            lic = gl_c[:, h:h + 1]
            dmat = jnp.where(causal, bc - br + lir, NEG_INF)
            m_loc = jnp.max(dmat, axis=-1, keepdims=True)
            qh, kh, vh = qc[:, hs], kc[:, hs], vc[:, hs]
            s = lax.dot_general(qh, kh, NT_DIMS, preferred_element_type=F32) * jnp.exp(dmat - m_loc)
            v_aug = jnp.concatenate([vh, ones_col], axis=1)
            intra_scr[j, h] = jnp.dot(s.astype(BF16), v_aug, preferred_element_type=F32)
            b_last = bc[last:last + 1, :]
            log_w = b_last - bc + lic
            mw_loc = jnp.max(log_w, axis=0, keepdims=True)
            wv = (jnp.exp(log_w - mw_loc) * v_aug.astype(F32)).astype(BF16)
            upd_scr[j, h] = lax.dot_general(kh, wv, TN_DIMS, preferred_element_type=F32)
            col_scr[j, h, :, 0:1] = bc
            col_scr[j, h, :, 1:2] = m_loc
            one_scr[j, h, :, 0:1] = b_last
            one_scr[j, h, :, 1:2] = mw_loc
        return carry

    for j in range(nchunk):
        local_terms(j, 0)

    def recurrence(j, carry):
        jj = nchunk - 1 - j if reverse else j
        r0 = pl.multiple_of(jj * CHUNK, CHUNK)
        rows = pl.ds(r0, CHUNK)
        qc = q_src[rows, :]
        if not first_pass:
            hfc = hf_ref[rows, :]
            omc = om_ref[rows, :]
        for h in range(nh):
            hs = slice(h * hd, (h + 1) * hd)
            m_prev = m_scr[h:h + 1, 0:1]
            bc = col_scr[jj, h, :, 0:1]
            m_loc = col_scr[jj, h, :, 1:2]
            b_last = one_scr[jj, h, :, 0:1]
            mw_loc = one_scr[jj, h, :, 1:2]
            inter = bc + m_prev
            m_t = jnp.maximum(inter, m_loc)
            c_old = c_scr[h]
            h_aug = (jnp.exp(m_loc - m_t) * intra_scr[jj, h]
                     + jnp.exp(inter - m_t) * jnp.dot(qc[:, hs], c_old.astype(BF16), preferred_element_type=F32))
            num = h_aug[:, :hd]
            den = h_aug[:, hd:hd + 1]
            h_out = num / jnp.maximum(jnp.abs(den), jnp.exp(-m_t))
            m_new = jnp.maximum(b_last + m_prev, mw_loc)
            c_scr[h] = jnp.exp(b_last + m_prev - m_new) * c_old + jnp.exp(mw_loc - m_new) * upd_scr[jj, h]
            m_scr[h:h + 1, :] = jnp.broadcast_to(m_new, (1, LANES))
            if first_pass:
                h_ref[rows, hs] = h_out
            else:
                y = jax.nn.sigmoid(omc[:, hs]) * (hfc[:, hs] + h_out)
                y = y * lax.rsqrt(jnp.mean(y * y, axis=-1, keepdims=True) + RMS_EPS) * og_ref[:, hs]
                h_ref[rows, hs] = y
        return carry

    lax.fori_loop(0, nchunk, recurrence, 0)


def _mlstm_pass(reverse, first_pass, operands, groups):
    mw, nh, hd = MLSTM_WIDTH, MLSTM_HEADS, MLSTM_HEAD_DIM
    tb = TILE
    T = operands[0].shape[0]
    nblk = T // tb
    ngate = 2 * nh
    bidx = (lambda s: nblk - 1 - s) if reverse else (lambda s: s)
    rows = lambda w: pl.BlockSpec((tb, w), lambda s: (bidx(s), 0))
    const = lambda a: pl.BlockSpec(a.shape, lambda s: (0,) * a.ndim)
    gate_rows = pl.BlockSpec((tb // CHUNK, ngate, CHUNK), lambda s: (bidx(s), 0, 0))
    hb = tb // SUBLANES
    nhb = T // SUBLANES
    nchunk = tb // CHUNK
    state = [pltpu.VMEM((nh, hd, 2 * hd), F32), pltpu.VMEM((SUBLANES, LANES), F32),
             pltpu.VMEM((nchunk, nh, CHUNK, 2 * hd), F32), pltpu.VMEM((nchunk, nh, hd, 2 * hd), F32),
             pltpu.VMEM((nchunk, nh, CHUNK, 2), F32), pltpu.VMEM((nchunk, nh, 1, 2), F32)]
    if first_pass:
        qk, cw, cb, v, gc, gr, bc, br = operands
        prev = pl.BlockSpec((SUBLANES, 2 * mw), lambda s: (jnp.maximum(bidx(s) * hb - 1, 0), 0))
        nxt = pl.BlockSpec((SUBLANES, 2 * mw), lambda s: (jnp.minimum((bidx(s) + 1) * hb, nhb - 1), 0))
        in_specs = [rows(2 * mw), prev, nxt, const(cw), const(cb), rows(mw), rows(LANES), gate_rows,
                    const(bc), const(br)]
        args = (qk, qk, qk, cw, cb, v, gc, gr, bc, br)
        out_specs = [rows(mw), rows(mw), rows(mw)]
        out_shape = [jax.ShapeDtypeStruct((T, mw), F32), jax.ShapeDtypeStruct((T, mw), BF16),
                     jax.ShapeDtypeStruct((T, mw), BF16)]
        scratch = state + [pltpu.VMEM((tb, mw), BF16), pltpu.VMEM((tb, mw), BF16),
                           pltpu.VMEM((tb + 2 * SUBLANES, 2 * mw), F32)]
    else:
        q, k, v, gc, gr, bc, br, hf, om, og = operands
        in_specs = [rows(mw), rows(mw), rows(mw), rows(LANES), gate_rows, const(bc), const(br),
                    rows(mw), rows(mw), const(og)]
        args = operands
        out_specs = rows(mw)
        out_shape = jax.ShapeDtypeStruct((T, mw), F32)
        scratch = state
    return pl.pallas_call(
        functools.partial(_mlstm_kernel, tb=tb, reverse=reverse, first_pass=first_pass, groups=groups),
        grid=(nblk,),
        in_specs=in_specs,
        out_specs=out_specs,
        out_shape=out_shape,
        scratch_shapes=scratch,
        compiler_params=_params("arbitrary"),
        name="mlstm_fwd" if first_pass else "mlstm_bwd",
    )(*args)


def _outproj_kernel(o1_ref, o4_ref, o16_ref, l1_ref, l4_ref, l16_ref, ml_ref, x_ref, ag_ref, bd_ref,
                    p4t_ref, p16t_ref, ex_ref, w_ref, g2_ref, rw_ref, rb_ref, tri_ref,
                    y_ref, h_ref, route_ref, cnt_ref, base_scr):
    unperm = lambda p_ref, o_ref: jnp.dot(p_ref[...], o_ref[...], preferred_element_type=F32)
    o1 = o1_ref[...].astype(F32)
    o4 = unperm(p4t_ref, o4_ref)
    o16 = unperm(p16t_ref, o16_ref)

    def unperm_f32(p_ref, l_ref):
        return _dot_split_rhs(p_ref[...], l_ref[...], 2)

    l1 = l1_ref[...]
    l4 = unperm_f32(p4t_ref, l4_ref)
    l16 = unperm_f32(p16t_ref, l16_ref)
    mx = jnp.maximum(jnp.maximum(l1, l4), l16)
    e1, e4, e16 = jnp.exp(l1 - mx), jnp.exp(l4 - mx), jnp.exp(l16 - mx)
    inv = 1.0 / (e1 + e4 + e16)
    expand = lambda w: _dot_split(w, ex_ref[...], 2)
    attn = o1 + expand(e4 * inv) * (o4 - o1) + expand(e16 * inv) * (o16 - o1)
    ss = _group_sumsq(attn, bd_ref)
    attn = attn * lax.rsqrt(ss * (1.0 / ATTN_HEAD_DIM) + RMS_EPS) * ag_ref[...]
    aw = ATTN_WIDTH
    y = jnp.dot(attn.astype(BF16), w_ref[:aw, :], preferred_element_type=F32)
    y = y + jnp.dot(ml_ref[...].astype(BF16), w_ref[aw:, :], preferred_element_type=F32)
    x = x_ref[...] + y
    y_ref[...] = x
    h = x * lax.rsqrt(jnp.mean(x * x, axis=-1, keepdims=True) + RMS_EPS) * g2_ref[...]
    h_ref[...] = h.astype(BF16)
    logits = _dot_f32(h, rw_ref[...]) + rb_ref[...]

    @pl.when(pl.program_id(0) == 0)
    def _():
        base_scr[...] = jnp.zeros_like(base_scr)

    tm = logits.shape[0]
    lane = lax.broadcasted_iota(jnp.int32, (tm, N_EXPERTS), 1).astype(F32)
    slot = lax.broadcasted_iota(jnp.int32, (tm, LANES), 1)
    work = logits
    vals, sels = [], []
    route = jnp.zeros((tm, LANES), F32)
    for r in range(TOP_K):
        mx_r = jnp.max(work, axis=-1, keepdims=True)
        idx_r = jnp.min(jnp.where(work == mx_r, lane, float(N_EXPERTS)), axis=-1, keepdims=True)
        sel = lane == idx_r
        work = jnp.where(sel, -jnp.inf, work)
        vals.append(mx_r)
        sels.append(sel)
        route = jnp.where(slot == r, idx_r, route)
    exps = [jnp.exp(v - vals[0]) for v in vals]
    inv_den = 1.0 / sum(exps)
    for r in range(TOP_K):
        route = jnp.where(slot == TOP_K + r, exps[r] * inv_den, route)
    onehot = sum(s.astype(F32) for s in sels)
    incl = jnp.dot(tri_ref[...], onehot.astype(BF16), preferred_element_type=F32)
    before = base_scr[0:1, 0:N_EXPERTS] + incl - onehot
    for r in range(TOP_K):
        rank_r = jnp.sum(jnp.where(sels[r], before, 0.0), axis=-1, keepdims=True)
        route = jnp.where(slot == 2 * TOP_K + r, rank_r, route)
    total = base_scr[0:1, 0:N_EXPERTS] + incl[tm - 1:tm, :]
    base_scr[0:1, 0:N_EXPERTS] = total
    route_ref[...] = route
    cnt_ref[...] = total.astype(jnp.int32)


def _outproj(os_, ls_, ml, x2, lw, consts):
    T = x2.shape[0]
    tm = TILE
    row = lambda w: pl.BlockSpec((tm, w), lambda i: (i, 0))
    const = lambda a: pl.BlockSpec(a.shape, lambda i: (0, 0))
    aw = ATTN_WIDTH
    cs = (lw['ag'], consts['bd'], consts['p4t'], consts['p16t'], consts['expand'], lw['w_out'],
          lw['g2'], lw['rw'], lw['rb'], consts['tri'])
    return pl.pallas_call(
        _outproj_kernel,
        grid=(T // tm,),
        in_specs=[row(aw)] * 3 + [row(LANES)] * 3 + [row(MLSTM_WIDTH), row(D_MODEL)] + [const(a) for a in cs],
        out_specs=[row(D_MODEL), row(D_MODEL), row(LANES), pl.BlockSpec((1, N_EXPERTS), lambda i: (0, 0))],
        out_shape=[jax.ShapeDtypeStruct((T, D_MODEL), F32), jax.ShapeDtypeStruct((T, D_MODEL), BF16),
                   jax.ShapeDtypeStruct((T, LANES), F32), jax.ShapeDtypeStruct((1, N_EXPERTS), jnp.int32)],
        scratch_shapes=[pltpu.VMEM((SUBLANES, LANES), F32)],
        compiler_params=_params("arbitrary"),
        name="outproj",
    )(*os_, *ls_, ml, x2, *cs)


def _expert_kernel(blk_ref, exp_ref, lo_ref, hi_ref, x_ref, wu_ref, bu_ref, wd_ref, bd_ref, y_ref,
                   wu_scr, wd_scr, *, bm):
    v = pl.program_id(0)
    pv = jnp.maximum(v - 1, 0)
    new_expert = (v == 0) | (exp_ref[v] != exp_ref[pv])
    first_visit = (v == 0) | (blk_ref[v] != blk_ref[pv])
    cast_rows = 128

    @pl.when(new_expert)
    def _():
        def cast(i, c):
            r = pl.ds(pl.multiple_of(i * cast_rows, cast_rows), cast_rows)
            wu_scr[r, :] = wu_ref[0, 0, r, :].astype(BF16)
            wd_scr[r, :] = wd_ref[0, 0, r, :].astype(BF16)
            return c
        lax.fori_loop(0, D_MODEL // cast_rows, cast, 0)

    lo, hi = lo_ref[v], hi_ref[v]

    @pl.when(hi > lo)
    def _():
        x = x_ref[...]
        half = D_FF // 2
        y = jnp.zeros((bm, D_MODEL), F32) + bd_ref[0, 0]
        for c in range(2):
            g0, l0 = c * half, D_FF + c * half
            x_glu = jnp.dot(x, wu_scr[:, g0:g0 + half], preferred_element_type=F32) + bu_ref[0, 0, :, g0:g0 + half]
            x_lin = jnp.dot(x, wu_scr[:, l0:l0 + half], preferred_element_type=F32) + bu_ref[0, 0, :, l0:l0 + half]
            x_glu = jnp.minimum(x_glu, SWIGLU_LIMIT)
            x_lin = jnp.clip(x_lin, -SWIGLU_LIMIT, SWIGLU_LIMIT)
            act = x_glu * jax.nn.sigmoid(SWIGLU_ALPHA * x_glu) * (x_lin + 1.0)
            y = y + jnp.dot(act.astype(BF16), wd_scr[g0:g0 + half, :], preferred_element_type=F32)
        rows = blk_ref[v] * bm + lax.broadcasted_iota(jnp.int32, (bm, 1), 0)
        mine = (rows >= lo) & (rows < hi)

        @pl.when(first_visit)
        def _():
            y_ref[...] = jnp.where(mine, y, 0.0).astype(y_ref.dtype)

        @pl.when(jnp.logical_not(first_visit))
        def _():
            y_ref[...] = jnp.where(mine, y.astype(y_ref.dtype), y_ref[...])


def _experts(meta, x_sorted, w_up, b_up, w_down, b_down, layer):
    A = x_sorted.shape[0]
    bm = EXPERT_ROWS
    nvisit = meta[0].shape[0]
    grid_spec = pltpu.PrefetchScalarGridSpec(
        num_scalar_prefetch=4,
        grid=(nvisit,),
        in_specs=[pl.BlockSpec((bm, D_MODEL), lambda v, blk, ex, lo, hi: (blk[v], 0)),
                  pl.BlockSpec((1, 1, D_MODEL, 2 * D_FF), lambda v, blk, ex, lo, hi: (layer, ex[v], 0, 0)),
                  pl.BlockSpec((1, 1, 1, 2 * D_FF), lambda v, blk, ex, lo, hi: (layer, ex[v], 0, 0)),
                  pl.BlockSpec((1, 1, D_FF, D_MODEL), lambda v, blk, ex, lo, hi: (layer, ex[v], 0, 0)),
                  pl.BlockSpec((1, 1, 1, D_MODEL), lambda v, blk, ex, lo, hi: (layer, ex[v], 0, 0))],
        out_specs=pl.BlockSpec((bm, D_MODEL), lambda v, blk, ex, lo, hi: (blk[v], 0)),
        scratch_shapes=[pltpu.VMEM((D_MODEL, 2 * D_FF), BF16), pltpu.VMEM((D_FF, D_MODEL), BF16)],
    )
    return pl.pallas_call(
        functools.partial(_expert_kernel, bm=bm),
        grid_spec=grid_spec,
        out_shape=jax.ShapeDtypeStruct((A, D_MODEL), BF16),
        compiler_params=_params("arbitrary", vmem=VMEM_LIMIT_EXPERTS),
        name="experts",
    )(*meta, x_sorted, w_up, b_up[:, :, None, :], w_down, b_down[:, :, None, :])


def _combine_kernel(x_ref, y0_ref, y1_ref, y2_ref, y3_ref, route_ref, o_ref):
    acc = x_ref[...]
    g = route_ref[...]
    for k, y_ref in enumerate((y0_ref, y1_ref, y2_ref, y3_ref)):
        acc = acc + y_ref[...].astype(F32) * g[:, TOP_K + k:TOP_K + k + 1]
    o_ref[...] = acc


def _combine(x2, ys, route):
    T = x2.shape[0]
    tm = TILE
    row = lambda w: pl.BlockSpec((tm, w), lambda i: (i, 0))
    return pl.pallas_call(
        _combine_kernel,
        grid=(T // tm,),
        in_specs=[row(D_MODEL)] * 5 + [row(LANES)],
        out_specs=row(D_MODEL),
        out_shape=jax.ShapeDtypeStruct((T, D_MODEL), F32),
        compiler_params=_params("parallel"),
        name="combine",
    )(x2, *ys, route)


def _visit_schedule(counts, nblk, bm):
    ends = jnp.cumsum(counts)
    starts = ends - counts
    first_blk = starts // bm
    nvis = jnp.where(counts > 0, (ends - 1) // bm - first_blk + 1, 0)
    vis_end = jnp.cumsum(nvis)
    vis_start = vis_end - nvis
    total = vis_end[-1]
    v = jnp.arange(nblk + N_EXPERTS - 1, dtype=jnp.int32)
    e = jnp.sum((vis_end[None, :] <= v[:, None]).astype(jnp.int32), axis=1)
    e = jnp.clip(e, 0, N_EXPERTS - 1)
    valid = v < total
    e_last = jnp.max(jnp.where(counts > 0, jnp.arange(N_EXPERTS, dtype=jnp.int32), 0))
    e = jnp.where(valid, e, e_last)
    blk = jnp.where(valid, first_blk[e] + v - vis_start[e], nblk - 1)
    lo = jnp.where(valid, starts[e], 0)
    hi = jnp.where(valid, ends[e], 0)
    i32 = lambda a: a.astype(jnp.int32)
    return i32(blk), i32(e), i32(lo), i32(hi)


def _moe(x2, h, route, counts, w_up, b_up, w_down, b_down, layer):
    T = x2.shape[0]
    A = T * TOP_K
    bm = EXPERT_ROWS
    assert A % bm == 0
    top_idx = route[:, :TOP_K].astype(jnp.int32)
    rank = route[:, 2 * TOP_K:3 * TOP_K].astype(jnp.int32)
    counts = counts[0]
    starts = jnp.cumsum(counts) - counts
    experts = jnp.arange(N_EXPERTS, dtype=jnp.int32)
    start_of = jnp.sum(jnp.where(top_idx[..., None] == experts, starts, 0), axis=-1)
    inv4 = start_of + rank
    iota = jnp.arange(A, dtype=jnp.int32)
    _, order = lax.sort((inv4.reshape(-1), iota), num_keys=1)
    x_sorted = jnp.take(h, order // TOP_K, axis=0, mode='clip')
    meta = _visit_schedule(counts, A // bm, bm)
    y_sorted = _experts(meta, x_sorted, w_up, b_up, w_down, b_down, layer)
    ys = [jnp.take(y_sorted, inv4[:, k], axis=0, mode='clip') for k in range(TOP_K)]
    return _combine(x2, ys, route)


def _rotary_tables(seq):
    half = ROT_DIM // 2
    inv_freq = ROPE_THETA ** (-jnp.arange(0, ROT_DIM, 2, dtype=F32) / ROT_DIM)
    ang = jnp.arange(seq).astype(F32)[:, None] * inv_freq[None, :]
    cos, sin = jnp.cos(ang), jnp.sin(ang)
    rest = ATTN_HEAD_DIM - ROT_DIM
    ones = jnp.ones((seq, rest), F32)
    zeros = jnp.zeros((seq, rest), F32)
    zh = jnp.zeros((seq, half), F32)
    cos_a = jnp.concatenate([cos, cos, ones], axis=1)
    sin_b = jnp.concatenate([-sin, zh, zeros], axis=1)
    sin_c = jnp.concatenate([zh, sin, zeros], axis=1)
    tile = lambda a: jnp.tile(a, (1, LANES // ATTN_HEAD_DIM))
    return tile(cos_a), tile(sin_b), tile(sin_c)


def _constants():
    idx = jnp.arange(ATTN_WIDTH) // ATTN_HEAD_DIM
    bd = (idx[:, None] == idx[None, :]).astype(BF16)

    def perm(d):
        new = jnp.arange(TILE)
        old = (new % (TILE // d)) * d + new // (TILE // d)
        return (old[:, None] == jnp.arange(TILE)[None, :]).astype(BF16)

    p4, p16 = perm(4), perm(16)
    expand = (jnp.arange(LANES)[:, None] == idx[None, :]).astype(BF16)
    tri = (jnp.arange(TILE)[:, None] >= jnp.arange(TILE)[None, :]).astype(BF16)
    return dict(bd=bd, p4=p4, p16=p16, p4t=p4.T, p16t=p16.T, expand=expand, tri=tri)


def _layer_weights(l, norm1_g, w_in, q_norm_g, k_norm_g, attn_out_g, conv_w, conv_b, igate_b, fgate_b,
                   mlstm_out_g, w_out, norm2_g, router_w, router_b):
    aw, mw, nh = ATTN_WIDTH, MLSTM_WIDTH, MLSTM_HEADS
    n_main = 3 * aw + 4 * mw
    w = w_in[l]
    wg = w[:, n_main:]
    pick = lambda a, d: jnp.concatenate([a[..., d * nh:(d + 1) * nh],
                                         a[..., (2 + d) * nh:(3 + d) * nh]], axis=-1)
    gate_bias = [jnp.concatenate([igate_b[l, d], fgate_b[l, d]]) for d in range(2)]
    return dict(
        g1=norm1_g[l][None, :], w_main=w[:, :n_main].astype(BF16),
        w_gate=jnp.pad(jnp.concatenate([pick(wg, 0), pick(wg, 1)], axis=1), ((0, 0), (0, LANES - 4 * nh))),
        qg=jnp.tile(q_norm_g[l], ATTN_HEADS)[None, :] * (ATTN_HEAD_DIM ** -0.5),
        kg=jnp.tile(k_norm_g[l], ATTN_HEADS)[None, :],
        ag=attn_out_g[l][None, :], cw=conv_w[l], cb=conv_b[l][None, :],
        bias_c=[b[None, :] for b in gate_bias], bias_r=[b[:, None] for b in gate_bias],
        og=mlstm_out_g[l][None, :], w_out=w_out[l].astype(BF16), g2=norm2_g[l][None, :],
        rw=router_w[l], rb=router_b[l][None, :])


def kernel(x_prompt, x_sample, norm1_g, w_in, q_norm_g, k_norm_g, attn_out_g, conv_w, conv_b, igate_b,
           fgate_b, mlstm_out_g, w_out, norm2_g, router_w, router_b, w_up, b_up, w_down, b_down):
    assert all(w // (2 * d) == HALO for w, d in SEGMENTS)
    groups = (x_prompt.shape[:2], x_sample.shape[:2])
    span = max(d for _, d in SEGMENTS) * LQ
    assert all(s % span == 0 for _, s in groups), groups
    small = (norm1_g, w_in, q_norm_g, k_norm_g, attn_out_g, conv_w, conv_b, igate_b, fgate_b,
             mlstm_out_g, w_out, norm2_g, router_w, router_b)
    consts = _constants()
    tables = _rotary_tables(max(s for _, s in groups))
    ngate = 2 * MLSTM_HEADS

    def layer(x2, lw, l, grp):
        chunk_rows = lambda d: jnp.swapaxes(
            gt[:, d * ngate:(d + 1) * ngate].reshape(x2.shape[0] // CHUNK, CHUNK, ngate), 1, 2)
        (q1, k1, v1, q4, k4, v4, q16, k16, v16, qkm, vm, om, gt) = _inproj(x2, lw, tables, consts, grp)
        segs = [_attn_segment(q1, k1, v1, 1, grp), _attn_segment(q4, k4, v4, 4, grp),
                _attn_segment(q16, k16, v16, 16, grp)]
        hf, qm, km = _mlstm_pass(False, True, (qkm, lw['cw'], lw['cb'], vm, gt, chunk_rows(0),
                                               lw['bias_c'][0], lw['bias_r'][0]), grp)
        ml = _mlstm_pass(True, False, (qm, km, vm, gt, chunk_rows(1), lw['bias_c'][1], lw['bias_r'][1],
                                       hf, om, lw['og']), grp)
        x2, h, route, counts = _outproj([s[0] for s in segs], [s[1] for s in segs], ml, x2, lw, consts)
        return _moe(x2, h, route, counts, w_up, b_up, w_down, b_down, l)

    (b1, s1), (b2, s2) = groups
    t1 = b1 * s1
    n_a = min(max(round(((t1 + b2 * s2) / 2 - t1) / s2), 0), b2)
    xp = x_prompt.reshape(-1, D_MODEL)
    xs = x_sample.reshape(-1, D_MODEL)
    streams = [(jnp.concatenate([xp, xs[:n_a * s2]], axis=0), ((b1, s1), (n_a, s2)))]
    if n_a < b2:
        streams.append((xs[n_a * s2:], ((b2 - n_a, s2), (0, s2))))
    for l in range(DEPTH):
        lw = _layer_weights(l, *small)
        streams = [(layer(x2, lw, l, grp), grp) for x2, grp in streams]
    outs = [x2 for x2, _ in streams]
    y_sample = jnp.concatenate([outs[0][t1:]] + outs[1:], axis=0)
    return (outs[0][:t1].reshape(x_prompt.shape), y_sample.reshape(x_sample.shape))
```

```python
import functools

import jax
import jax.numpy as jnp
from jax import lax
from jax.experimental import pallas as pl
from jax.experimental.pallas import tpu as pltpu

D_MODEL = 1024
DEPTH = 2
ATTN_WIDTH = 512
ATTN_HEAD_DIM = 64
ATTN_HEADS = 8
ROT_DIM = 16
ROPE_THETA = 500000.0
SEGMENTS = ((128, 1), (512, 4), (2048, 16))
HALO = 64
MLSTM_WIDTH = 512
MLSTM_HEAD_DIM = 128
MLSTM_HEADS = 4
CONV_WIDTH = 5
CHUNK = 64
N_EXPERTS = 32
TOP_K = 4
D_FF = 1024
SWIGLU_LIMIT = 7.0
SWIGLU_ALPHA = 1.702
RMS_EPS = 1e-6
NEG_INF = -1e30

LANES = 128
SUBLANES = 8
TILE = 512
LQ = 128
EXPERT_ROWS = 512
VMEM_LIMIT = 48 * 1024 * 1024
VMEM_LIMIT_EXPERTS = 56 * 1024 * 1024

F32 = jnp.float32
BF16 = jnp.bfloat16
NT_DIMS = (((1,), (1,)), ((), ()))
TN_DIMS = (((0,), (0,)), ((), ()))


def _params(*sem, vmem=VMEM_LIMIT):
    return pltpu.CompilerParams(dimension_semantics=sem, vmem_limit_bytes=vmem)


def _seq_bounds(row, groups):
    (b1, s1), (_, s2) = groups
    t1 = b1 * s1
    in1 = row < t1
    start = jnp.where(in1, (row // s1) * s1, t1 + ((row - t1) // s2) * s2)
    return start, start + jnp.where(in1, s1, s2)


def _split3(a):
    hi = a.astype(BF16)
    r1 = a - hi.astype(F32)
    mid = r1.astype(BF16)
    lo = (r1 - mid.astype(F32)).astype(BF16)
    return hi, mid, lo


def _dot_split(a, b, pieces):
    parts = _split3(a)[:pieces]
    out = jnp.dot(parts[0], b, preferred_element_type=F32)
    for p in parts[1:]:
        out = out + jnp.dot(p, b, preferred_element_type=F32)
    return out


def _dot_split_rhs(a, b, pieces):
    parts = _split3(b)[:pieces]
    out = jnp.dot(a, parts[0], preferred_element_type=F32)
    for p in parts[1:]:
        out = out + jnp.dot(a, p, preferred_element_type=F32)
    return out


def _dot_f32(a, b):
    ah, al, _ = _split3(a)
    bh, bl, _ = _split3(b)
    out = jnp.dot(ah, bh, preferred_element_type=F32)
    out = out + jnp.dot(ah, bl, preferred_element_type=F32)
    return out + jnp.dot(al, bh, preferred_element_type=F32)


def _group_sumsq(z, bd_ref):
    return _dot_split(z * z, bd_ref[...], 1)


def _inproj_kernel(x_ref, g1_ref, w_ref, wg_ref, qg_ref, kg_ref, cos_ref, sinb_ref, sinc_ref,
                   bd_ref, p4_ref, p16_ref,
                   q1_ref, k1_ref, v1_ref, q4_ref, k4_ref, v4_ref, q16_ref, k16_ref, v16_ref,
                   qkm_ref, vm_ref, om_ref, gf_ref, gb_ref):
    x = x_ref[...]
    xn = x * lax.rsqrt(jnp.mean(x * x, axis=-1, keepdims=True) + RMS_EPS) * g1_ref[...]
    xb = xn.astype(BF16)
    aw, mw = ATTN_WIDTH, MLSTM_WIDTH

    def proj(lo, hi):
        return jnp.dot(xb, w_ref[:, lo:hi], preferred_element_type=F32)

    reps = aw // LANES
    cos_a = jnp.concatenate([cos_ref[...]] * reps, axis=1)
    sin_b = jnp.concatenate([sinb_ref[...]] * reps, axis=1)
    sin_c = jnp.concatenate([sinc_ref[...]] * reps, axis=1)

    def norm_rot(z, g_ref):
        ss = _group_sumsq(z, bd_ref)
        y = z * lax.rsqrt(ss * (1.0 / ATTN_HEAD_DIM) + RMS_EPS) * g_ref[...]
        half = ROT_DIM // 2
        y_up = pltpu.roll(y, aw - half, axis=1)
        y_dn = pltpu.roll(y, half, axis=1)
        return (y * cos_a + y_up * sin_b + y_dn * sin_c).astype(BF16)

    def emit(z, nat_ref, r4_ref, r16_ref):
        nat_ref[...] = z
        r4_ref[...] = jnp.dot(p4_ref[...], z, preferred_element_type=F32).astype(BF16)
        r16_ref[...] = jnp.dot(p16_ref[...], z, preferred_element_type=F32).astype(BF16)

    emit(norm_rot(proj(0, aw), qg_ref), q1_ref, q4_ref, q16_ref)
    emit(norm_rot(proj(aw, 2 * aw), kg_ref), k1_ref, k4_ref, k16_ref)
    emit(proj(2 * aw, 3 * aw).astype(BF16), v1_ref, v4_ref, v16_ref)
    qkm_ref[...] = proj(3 * aw, 3 * aw + 2 * mw)
    vm_ref[...] = proj(3 * aw + 2 * mw, 3 * aw + 3 * mw).astype(BF16)
    om_ref[...] = proj(3 * aw + 3 * mw, 3 * aw + 4 * mw)
    gates = _dot_f32(xn, wg_ref[...])
    ng = 2 * MLSTM_HEADS
    gf_ref[...] = gates[:, :ng]
    gb_ref[...] = gates[:, ng:]


def _inproj(x2, lw, tables, consts, groups):
    T = x2.shape[0]
    tm = TILE
    aw, mw, ng = ATTN_WIDTH, MLSTM_WIDTH, 2 * MLSTM_HEADS
    row = lambda w: pl.BlockSpec((tm, w), lambda i: (i, 0))
    const = lambda a: pl.BlockSpec(a.shape, lambda i: (0, 0))

    def tab_index(i):
        start, _ = _seq_bounds(i * tm, groups)
        return (i - start // tm, 0)

    tab = pl.BlockSpec((tm, LANES), tab_index)
    weights = (lw['g1'], lw['w_main'], lw['w_gate'], lw['qg'], lw['kg'])
    perms = (consts['bd'], consts['p4'], consts['p16'])
    bf = lambda w: jax.ShapeDtypeStruct((T, w), BF16)
    f32 = lambda w: jax.ShapeDtypeStruct((T, w), F32)
    return pl.pallas_call(
        _inproj_kernel,
        grid=(T // tm,),
        in_specs=[row(D_MODEL)] + [const(a) for a in weights] + [tab, tab, tab] + [const(a) for a in perms],
        out_specs=[row(aw)] * 9 + [row(2 * mw), row(mw), row(mw), row(ng), row(ng)],
        out_shape=[bf(aw)] * 9 + [f32(2 * mw), bf(mw), f32(mw), f32(ng), f32(ng)],
        compiler_params=_params("parallel"),
        name="inproj",
    )(x2, *weights, *tables, *perms)


def _attn_kernel(q_ref, kl_ref, kc_ref, kr_ref, vl_ref, vc_ref, vr_ref, o_ref, lse_ref, *,
                 dilation, groups):
    aw = ATTN_WIDTH
    lq, lk = LQ, LQ + 2 * HALO
    row0 = pl.program_id(0) * (LQ * dilation)
    start, end = _seq_bounds(row0, groups)
    m0 = (row0 - start) // dilation
    sub_len = (end - start) // dilation

    flat = lambda ref, n: ref[...].reshape(n, ref.shape[-1])
    q = flat(q_ref, lq)
    k = jnp.concatenate([flat(kl_ref, HALO), flat(kc_ref, lq), flat(kr_ref, HALO)], axis=0)
    v = jnp.concatenate([flat(vl_ref, HALO), flat(vc_ref, lq), flat(vr_ref, HALO)], axis=0)

    row = lax.broadcasted_iota(jnp.int32, (2 * lq, lk), 0)
    col = lax.broadcasted_iota(jnp.int32, (2 * lq, lk), 1)
    qrow = jnp.where(row >= lq, row - lq, row)
    rel = col - HALO - qrow
    kpos = m0 - HALO + col
    mask = (jnp.abs(rel) <= HALO) & (kpos >= 0) & (kpos < sub_len)

    lane2 = lax.broadcasted_iota(jnp.int32, (2 * lq, LANES), 1)
    row2 = lax.broadcasted_iota(jnp.int32, (2 * lq, LANES), 0)
    own = (lane2 >= ATTN_HEAD_DIM) ^ (row2 < lq)
    lane = lax.broadcasted_iota(jnp.int32, (lq, LANES), 1)
    first = lane < ATTN_HEAD_DIM

    lse_tile = jnp.zeros((lq, LANES), F32)
    outs = []
    for hp in range(aw // LANES):
        sl = slice(hp * LANES, (hp + 1) * LANES)
        qp = q[:, sl]
        q2 = jnp.concatenate([qp, qp], axis=0)
        q2 = jnp.where(own, q2, jnp.zeros_like(q2))
        s = lax.dot_general(q2, k[:, sl], NT_DIMS, preferred_element_type=F32)
        s = jnp.where(mask, s, NEG_INF)
        mx = jnp.max(s, axis=-1, keepdims=True)
        p = jnp.exp(s - mx)
        den = jnp.sum(p, axis=-1, keepdims=True)
        pv = jnp.dot(p.astype(BF16), v[:, sl], preferred_element_type=F32)
        o2 = pv / den
        lse2 = mx + jnp.log(den)
        outs.append(jnp.where(first, o2[:lq], o2[lq:]).astype(BF16))
        lse_tile = jnp.where(lane == 2 * hp, lse2[:lq], lse_tile)
        lse_tile = jnp.where(lane == 2 * hp + 1, lse2[lq:], lse_tile)
    o_ref[...] = jnp.concatenate(outs, axis=1).reshape(o_ref.shape)
    lse_ref[...] = lse_tile.reshape(lse_ref.shape)


def _attn_segment(q, k, v, dilation, groups):
    T, aw = q.shape
    tm = TILE
    per_tile = tm // dilation
    ntile = T // tm
    if per_tile >= LQ:
        nblk = per_tile // LQ
        nhalo = per_tile // HALO
        hb = LQ // HALO
        shape = lambda w: (ntile * dilation, per_tile, w)
        grid = (ntile * nblk, dilation)
        slab = lambda i, r: (i // nblk) * dilation + r

        def cen(w):
            return pl.BlockSpec((1, LQ, w), lambda i, r: (slab(i, r), i % nblk, 0))

        def halo_spec(side):
            def index(i, r):
                j = (i % nblk) * hb + (-1 if side < 0 else hb)
                t = i // nblk + jnp.where(j < 0, -1, 0) + jnp.where(j >= nhalo, 1, 0)
                t = jnp.clip(t, 0, ntile - 1)
                return (t * dilation + r, j % nhalo, 0)
            return pl.BlockSpec((1, HALO, aw), index)
    else:
        tq = LQ // per_tile
        th = HALO // per_tile
        nhb = ntile // th
        shape = lambda w: (ntile, dilation, per_tile, w)
        grid = (ntile // tq, dilation)

        def cen(w):
            return pl.BlockSpec((tq, None, per_tile, w), lambda i, r: (i, r, 0, 0))

        def halo_spec(side):
            def index(i, r):
                j = i * (tq // th) + (-1 if side < 0 else tq // th)
                return (jnp.clip(j, 0, nhb - 1), r, 0, 0)
            return pl.BlockSpec((th, None, per_tile, aw), index)

    left, right = halo_spec(-1), halo_spec(1)
    view = lambda a: a.reshape(shape(a.shape[-1]))
    o, lse = pl.pallas_call(
        functools.partial(_attn_kernel, dilation=dilation, groups=groups),
        grid=grid,
        in_specs=[cen(aw), left, cen(aw), right, left, cen(aw), right],
        out_specs=[cen(aw), cen(LANES)],
        out_shape=[jax.ShapeDtypeStruct(shape(aw), BF16), jax.ShapeDtypeStruct(shape(LANES), F32)],
        compiler_params=_params("parallel", "parallel"),
        name=f"attn_d{dilation}",
    )(view(q), view(k), view(k), view(k), view(v), view(v), view(v))
    return o.reshape(T, aw), lse.reshape(T, LANES)


def _log_sigmoid(x):
    return jnp.minimum(x, 0.0) - jnp.log(1.0 + jnp.exp(-jnp.abs(x)))


def _mlstm_kernel(*refs, tb, reverse, first_pass, groups):
    nh, hd = MLSTM_HEADS, MLSTM_HEAD_DIM
    mw = MLSTM_WIDTH
    if first_pass:
        (qk_ref, qkp_ref, qkn_ref, cw_ref, cb_ref, v_ref, gc_ref, gr_ref, bc_ref, br_ref,
         h_ref, qo_ref, ko_ref, c_scr, m_scr, intra_scr, upd_scr, col_scr, one_scr,
         q_scr, k_scr, xe_scr) = refs
    else:
        (q_ref, k_ref, v_ref, gc_ref, gr_ref, bc_ref, br_ref, hf_ref, om_ref, og_ref,
         h_ref, c_scr, m_scr, intra_scr, upd_scr, col_scr, one_scr) = refs
    step = pl.program_id(0)
    blk = pl.num_programs(0) - 1 - step if reverse else step
    row0 = blk * tb
    start, end = _seq_bounds(row0, groups)
    at_start = row0 == start
    at_end = row0 + tb == end

    @pl.when(at_end if reverse else at_start)
    def _():
        c_scr[...] = jnp.zeros_like(c_scr)
        m_scr[...] = jnp.zeros_like(m_scr)

    if first_pass:
        pad = CONV_WIDTH // 2
        xe_scr[0:SUBLANES, :] = jnp.where(at_start, 0.0, qkp_ref[...])
        xe_scr[SUBLANES:SUBLANES + tb, :] = qk_ref[...]
        xe_scr[SUBLANES + tb:, :] = jnp.where(at_end, 0.0, qkn_ref[...])
        acc = jnp.zeros((tb, 2 * mw), F32) + cb_ref[...]
        for j in range(CONV_WIDTH):
            acc = acc + xe_scr[SUBLANES - pad + j:SUBLANES - pad + j + tb, :] * cw_ref[j:j + 1, :]
        act = acc * jax.nn.sigmoid(acc)
        qb = act[:, :mw].astype(BF16)
        kb = (act[:, mw:] * (hd ** -0.5)).astype(BF16)
        qo_ref[...] = qb
        ko_ref[...] = kb
        q_scr[...] = qb
        k_scr[...] = kb
        q_src, k_src = q_scr, k_scr
    else:
        q_src, k_src = q_ref, k_ref

    ti = lax.broadcasted_iota(jnp.int32, (CHUNK, CHUNK), 0)
    si = lax.broadcasted_iota(jnp.int32, (CHUNK, CHUNK), 1)
    causal = (si >= ti) if reverse else (si <= ti)
    cum_col = causal.astype(BF16)
    cum_row = ((ti >= si) if reverse else (ti <= si)).astype(BF16)
    last = 0 if reverse else CHUNK - 1
    ones_col = (lax.broadcasted_iota(jnp.int32, (CHUNK, hd), 1) == 0).astype(BF16)
    ngate = 2 * nh
    gc_is_i = lax.broadcasted_iota(jnp.int32, (CHUNK, ngate), 1) < nh
    gr_is_i = lax.broadcasted_iota(jnp.int32, (ngate, CHUNK), 0) < nh
    nchunk = tb // CHUNK

    def local_terms(j, carry):
        rows = pl.ds(j * CHUNK, CHUNK)
        a_c = gc_ref[rows, :] + bc_ref[...]
        a_r = gr_ref[j] + br_ref[...]
        gl_c = jnp.where(gc_is_i, a_c, _log_sigmoid(a_c))
        gl_r = jnp.where(gr_is_i, a_r, _log_sigmoid(a_r))
        b_c = _dot_split_rhs(cum_col, gl_c, 3)
        b_r = _dot_split(gl_r, cum_row, 3)
        qc = q_src[rows, :]
        kc = k_src[rows, :]
        vc = v_ref[rows, :]
        for h in range(nh):
            hs = slice(h * hd, (h + 1) * hd)
            bc = b_c[:, nh + h:nh + h + 1]
            br = b_r[nh + h:nh + h + 1, :]
            lir = gl_r[h:h + 1, :]
            lic = gl_c[:, h:h + 1]
            dmat = jnp.where(causal, bc - br + lir, NEG_INF)
            m_loc = jnp.max(dmat, axis=-1, keepdims=True)
            qh, kh, vh = qc[:, hs], kc[:, hs], vc[:, hs]
            s = lax.dot_general(qh, kh, NT_DIMS, preferred_element_type=F32) * jnp.exp(dmat - m_loc)
            v_aug = jnp.concatenate([vh, ones_col], axis=1)
            intra_scr[j, h] = jnp.dot(s.astype(BF16), v_aug, preferred_element_type=F32)
            b_last = bc[last:last + 1, :]
            log_w = b_last - bc + lic
            mw_loc = jnp.max(log_w, axis=0, keepdims=True)
            wv = (jnp.exp(log_w - mw_loc) * v_aug.astype(F32)).astype(BF16)
            upd_scr[j, h] = lax.dot_general(kh, wv, TN_DIMS, preferred_element_type=F32)
            col_scr[j, h, :, 0:1] = bc
            col_scr[j, h, :, 1:2] = m_loc
            one_scr[j, h, :, 0:1] = b_last
            one_scr[j, h, :, 1:2] = mw_loc
        return carry

    for j in range(nchunk):
        local_terms(j, 0)

    def recurrence(j, carry):
        jj = nchunk - 1 - j if reverse else j
        r0 = pl.multiple_of(jj * CHUNK, CHUNK)
        rows = pl.ds(r0, CHUNK)
        qc = q_src[rows, :]
        if not first_pass:
            hfc = hf_ref[rows, :]
            omc = om_ref[rows, :]
        for h in range(nh):
            hs = slice(h * hd, (h + 1) * hd)
            m_prev = m_scr[h:h + 1, 0:1]
            bc = col_scr[jj, h, :, 0:1]
            m_loc = col_scr[jj, h, :, 1:2]
            b_last = one_scr[jj, h, :, 0:1]
            mw_loc = one_scr[jj, h, :, 1:2]
            inter = bc + m_prev
            m_t = jnp.maximum(inter, m_loc)
            c_old = c_scr[h]
            h_aug = (jnp.exp(m_loc - m_t) * intra_scr[jj, h]
                     + jnp.exp(inter - m_t) * jnp.dot(qc[:, hs], c_old.astype(BF16), preferred_element_type=F32))
            num = h_aug[:, :hd]
            den = h_aug[:, hd:hd + 1]
            h_out = num / jnp.maximum(jnp.abs(den), jnp.exp(-m_t))
            m_new = jnp.maximum(b_last + m_prev, mw_loc)
            c_scr[h] = jnp.exp(b_last + m_prev - m_new) * c_old + jnp.exp(mw_loc - m_new) * upd_scr[jj, h]
            m_scr[h:h + 1, :] = jnp.broadcast_to(m_new, (1, LANES))
            if first_pass:
                h_ref[rows, hs] = h_out
            else:
                y = jax.nn.sigmoid(omc[:, hs]) * (hfc[:, hs] + h_out)
                y = y * lax.rsqrt(jnp.mean(y * y, axis=-1, keepdims=True) + RMS_EPS) * og_ref[:, hs]
                h_ref[rows, hs] = y
        return carry

    lax.fori_loop(0, nchunk, recurrence, 0)


def _mlstm_pass(reverse, first_pass, operands, groups):
    mw, nh, hd = MLSTM_WIDTH, MLSTM_HEADS, MLSTM_HEAD_DIM
    tb = TILE
    T = operands[0].shape[0]
    nblk = T // tb
    ngate = 2 * nh
    bidx = (lambda s: nblk - 1 - s) if reverse else (lambda s: s)
    rows = lambda w: pl.BlockSpec((tb, w), lambda s: (bidx(s), 0))
    const = lambda a: pl.BlockSpec(a.shape, lambda s: (0,) * a.ndim)
    gate_rows = pl.BlockSpec((tb // CHUNK, ngate, CHUNK), lambda s: (bidx(s), 0, 0))
    hb = tb // SUBLANES
    nhb = T // SUBLANES
    nchunk = tb // CHUNK
    state = [pltpu.VMEM((nh, hd, 2 * hd), F32), pltpu.VMEM((SUBLANES, LANES), F32),
             pltpu.VMEM((nchunk, nh, CHUNK, 2 * hd), F32), pltpu.VMEM((nchunk, nh, hd, 2 * hd), F32),
             pltpu.VMEM((nchunk, nh, CHUNK, 2), F32), pltpu.VMEM((nchunk, nh, 1, 2), F32)]
    if first_pass:
        qk, cw, cb, v, gc, gr, bc, br = operands
        prev = pl.BlockSpec((SUBLANES, 2 * mw), lambda s: (jnp.maximum(bidx(s) * hb - 1, 0), 0))
        nxt = pl.BlockSpec((SUBLANES, 2 * mw), lambda s: (jnp.minimum((bidx(s) + 1) * hb, nhb - 1), 0))
        in_specs = [rows(2 * mw), prev, nxt, const(cw), const(cb), rows(mw), rows(ngate), gate_rows,
                    const(bc), const(br)]
        args = (qk, qk, qk, cw, cb, v, gc, gr, bc, br)
        out_specs = [rows(mw), rows(mw), rows(mw)]
        out_shape = [jax.ShapeDtypeStruct((T, mw), F32), jax.ShapeDtypeStruct((T, mw), BF16),
                     jax.ShapeDtypeStruct((T, mw), BF16)]
        scratch = state + [pltpu.VMEM((tb, mw), BF16), pltpu.VMEM((tb, mw), BF16),
                           pltpu.VMEM((tb + 2 * SUBLANES, 2 * mw), F32)]
    else:
        q, k, v, gc, gr, bc, br, hf, om, og = operands
        in_specs = [rows(mw), rows(mw), rows(mw), rows(ngate), gate_rows, const(bc), const(br),
                    rows(mw), rows(mw), const(og)]
        args = operands
        out_specs = rows(mw)
        out_shape = jax.ShapeDtypeStruct((T, mw), F32)
        scratch = state
    return pl.pallas_call(
        functools.partial(_mlstm_kernel, tb=tb, reverse=reverse, first_pass=first_pass, groups=groups),
        grid=(nblk,),
        in_specs=in_specs,
        out_specs=out_specs,
        out_shape=out_shape,
        scratch_shapes=scratch,
        compiler_params=_params("arbitrary"),
        name="mlstm_fwd" if first_pass else "mlstm_bwd",
    )(*args)


def _outproj_kernel(o1_ref, o4_ref, o16_ref, l1_ref, l4_ref, l16_ref, ml_ref, x_ref, ag_ref, bd_ref,
                    p4t_ref, p16t_ref, ex_ref, w_ref, g2_ref, rw_ref, rb_ref, tri_ref,
                    y_ref, h_ref, idx_ref, gate_ref, rank_ref, cnt_ref, base_scr):
    unperm = lambda p_ref, o_ref: jnp.dot(p_ref[...], o_ref[...], preferred_element_type=F32)
    o1 = o1_ref[...].astype(F32)
    o4 = unperm(p4t_ref, o4_ref)
    o16 = unperm(p16t_ref, o16_ref)

    def unperm_f32(p_ref, l_ref):
        return _dot_split_rhs(p_ref[...], l_ref[...], 2)

    l1 = l1_ref[...]
    l4 = unperm_f32(p4t_ref, l4_ref)
    l16 = unperm_f32(p16t_ref, l16_ref)
    mx = jnp.maximum(jnp.maximum(l1, l4), l16)
    e1, e4, e16 = jnp.exp(l1 - mx), jnp.exp(l4 - mx), jnp.exp(l16 - mx)
    inv = 1.0 / (e1 + e4 + e16)
    expand = lambda w: _dot_split(w, ex_ref[...], 2)
    attn = o1 + expand(e4 * inv) * (o4 - o1) + expand(e16 * inv) * (o16 - o1)
    ss = _group_sumsq(attn, bd_ref)
    attn = attn * lax.rsqrt(ss * (1.0 / ATTN_HEAD_DIM) + RMS_EPS) * ag_ref[...]
    aw = ATTN_WIDTH
    y = jnp.dot(attn.astype(BF16), w_ref[:aw, :], preferred_element_type=F32)
    y = y + jnp.dot(ml_ref[...].astype(BF16), w_ref[aw:, :], preferred_element_type=F32)
    x = x_ref[...] + y
    y_ref[...] = x
    h = x * lax.rsqrt(jnp.mean(x * x, axis=-1, keepdims=True) + RMS_EPS) * g2_ref[...]
    h_ref[...] = h.astype(BF16)
    logits = _dot_f32(h, rw_ref[...]) + rb_ref[...]

    @pl.when(pl.program_id(0) == 0)
    def _():
        base_scr[...] = jnp.zeros_like(base_scr)

    tm = logits.shape[0]
    lane = lax.broadcasted_iota(jnp.int32, (tm, N_EXPERTS), 1).astype(F32)
    slot = lax.broadcasted_iota(jnp.int32, (tm, TOP_K), 1)
    work = logits
    vals, sels = [], []
    idx4 = jnp.zeros((tm, TOP_K), F32)
    for r in range(TOP_K):
        mx_r = jnp.max(work, axis=-1, keepdims=True)
        idx_r = jnp.min(jnp.where(work == mx_r, lane, float(N_EXPERTS)), axis=-1, keepdims=True)
        sel = lane == idx_r
        work = jnp.where(sel, -jnp.inf, work)
        vals.append(mx_r)
        sels.append(sel)
        idx4 = jnp.where(slot == r, idx_r, idx4)
    exps = [jnp.exp(v - vals[0]) for v in vals]
    inv_den = 1.0 / sum(exps)
    gate4 = jnp.zeros((tm, TOP_K), F32)
    for r in range(TOP_K):
        gate4 = jnp.where(slot == r, exps[r] * inv_den, gate4)
    onehot = sum(s.astype(F32) for s in sels)
    incl = jnp.dot(tri_ref[...], onehot.astype(BF16), preferred_element_type=F32)
    before = base_scr[0:1, 0:N_EXPERTS] + incl - onehot
    rank4 = jnp.zeros((tm, TOP_K), F32)
    for r in range(TOP_K):
        rank_r = jnp.sum(jnp.where(sels[r], before, 0.0), axis=-1, keepdims=True)
        rank4 = jnp.where(slot == r, rank_r, rank4)
    total = base_scr[0:1, 0:N_EXPERTS] + incl[tm - 1:tm, :]
    base_scr[0:1, 0:N_EXPERTS] = total
    idx_ref[...] = idx4.astype(jnp.int32)
    gate_ref[...] = gate4
    rank_ref[...] = rank4.astype(jnp.int32)
    cnt_ref[...] = total.astype(jnp.int32)


def _outproj(os_, ls_, ml, x2, lw, consts):
    T = x2.shape[0]
    tm = TILE
    row = lambda w: pl.BlockSpec((tm, w), lambda i: (i, 0))
    const = lambda a: pl.BlockSpec(a.shape, lambda i: (0, 0))
    aw = ATTN_WIDTH
    cs = (lw['ag'], consts['bd'], consts['p4t'], consts['p16t'], consts['expand'], lw['w_out'],
          lw['g2'], lw['rw'], lw['rb'], consts['tri'])
    i32 = jnp.int32
    return pl.pallas_call(
        _outproj_kernel,
        grid=(T // tm,),
        in_specs=[row(aw)] * 3 + [row(LANES)] * 3 + [row(MLSTM_WIDTH), row(D_MODEL)] + [const(a) for a in cs],
        out_specs=[row(D_MODEL), row(D_MODEL), row(TOP_K), row(TOP_K), row(TOP_K),
                   pl.BlockSpec((1, N_EXPERTS), lambda i: (0, 0))],
        out_shape=[jax.ShapeDtypeStruct((T, D_MODEL), F32), jax.ShapeDtypeStruct((T, D_MODEL), BF16),
                   jax.ShapeDtypeStruct((T, TOP_K), i32), jax.ShapeDtypeStruct((T, TOP_K), F32),
                   jax.ShapeDtypeStruct((T, TOP_K), i32), jax.ShapeDtypeStruct((1, N_EXPERTS), i32)],
        scratch_shapes=[pltpu.VMEM((SUBLANES, LANES), F32)],
        compiler_params=_params("arbitrary"),
        name="outproj",
    )(*os_, *ls_, ml, x2, *cs)


def _expert_kernel(blk_ref, exp_ref, lo_ref, hi_ref, x_ref, wu_ref, bu_ref, wd_ref, bd_ref, y_ref,
                   wu_scr, wd_scr, *, bm):
    v = pl.program_id(0)
    pv = jnp.maximum(v - 1, 0)
    new_expert = (v == 0) | (exp_ref[v] != exp_ref[pv])
    first_visit = (v == 0) | (blk_ref[v] != blk_ref[pv])
    cast_rows = 128

    @pl.when(new_expert)
    def _():
        def cast(i, c):
            r = pl.ds(pl.multiple_of(i * cast_rows, cast_rows), cast_rows)
            wu_scr[r, :] = wu_ref[0, 0, r, :].astype(BF16)
            wd_scr[r, :] = wd_ref[0, 0, r, :].astype(BF16)
            return c
        lax.fori_loop(0, D_MODEL // cast_rows, cast, 0)

    lo, hi = lo_ref[v], hi_ref[v]

    @pl.when(hi > lo)
    def _():
        x = x_ref[...]
        half = D_FF // 2
        y = jnp.zeros((bm, D_MODEL), F32) + bd_ref[0, 0]
        for c in range(2):
            g0, l0 = c * half, D_FF + c * half
            x_glu = jnp.dot(x, wu_scr[:, g0:g0 + half], preferred_element_type=F32) + bu_ref[0, 0, :, g0:g0 + half]
            x_lin = jnp.dot(x, wu_scr[:, l0:l0 + half], preferred_element_type=F32) + bu_ref[0, 0, :, l0:l0 + half]
            x_glu = jnp.minimum(x_glu, SWIGLU_LIMIT)
            x_lin = jnp.clip(x_lin, -SWIGLU_LIMIT, SWIGLU_LIMIT)
            act = x_glu * jax.nn.sigmoid(SWIGLU_ALPHA * x_glu) * (x_lin + 1.0)
            y = y + jnp.dot(act.astype(BF16), wd_scr[g0:g0 + half, :], preferred_element_type=F32)
        rows = blk_ref[v] * bm + lax.broadcasted_iota(jnp.int32, (bm, 1), 0)
        mine = (rows >= lo) & (rows < hi)

        @pl.when(first_visit)
        def _():
            y_ref[...] = jnp.where(mine, y, 0.0).astype(y_ref.dtype)

        @pl.when(jnp.logical_not(first_visit))
        def _():
            y_ref[...] = jnp.where(mine, y.astype(y_ref.dtype), y_ref[...])


def _experts(meta, x_sorted, w_up, b_up, w_down, b_down, layer):
    A = x_sorted.shape[0]
    bm = EXPERT_ROWS
    nvisit = meta[0].shape[0]
    grid_spec = pltpu.PrefetchScalarGridSpec(
        num_scalar_prefetch=4,
        grid=(nvisit,),
        in_specs=[pl.BlockSpec((bm, D_MODEL), lambda v, blk, ex, lo, hi: (blk[v], 0)),
                  pl.BlockSpec((1, 1, D_MODEL, 2 * D_FF), lambda v, blk, ex, lo, hi: (layer, ex[v], 0, 0)),
                  pl.BlockSpec((1, 1, 1, 2 * D_FF), lambda v, blk, ex, lo, hi: (layer, ex[v], 0, 0)),
                  pl.BlockSpec((1, 1, D_FF, D_MODEL), lambda v, blk, ex, lo, hi: (layer, ex[v], 0, 0)),
                  pl.BlockSpec((1, 1, 1, D_MODEL), lambda v, blk, ex, lo, hi: (layer, ex[v], 0, 0))],
        out_specs=pl.BlockSpec((bm, D_MODEL), lambda v, blk, ex, lo, hi: (blk[v], 0)),
        scratch_shapes=[pltpu.VMEM((D_MODEL, 2 * D_FF), BF16), pltpu.VMEM((D_FF, D_MODEL), BF16)],
    )
    return pl.pallas_call(
        functools.partial(_expert_kernel, bm=bm),
        grid_spec=grid_spec,
        out_shape=jax.ShapeDtypeStruct((A, D_MODEL), BF16),
        compiler_params=_params("arbitrary", vmem=VMEM_LIMIT_EXPERTS),
        name="experts",
    )(*meta, x_sorted, w_up, b_up[:, :, None, :], w_down, b_down[:, :, None, :])


def _combine_kernel(x_ref, y0_ref, y1_ref, y2_ref, y3_ref, g_ref, o_ref):
    acc = x_ref[...]
    g = g_ref[...]
    for k, y_ref in enumerate((y0_ref, y1_ref, y2_ref, y3_ref)):
        acc = acc + y_ref[...].astype(F32) * g[:, k:k + 1]
    o_ref[...] = acc


def _combine(x2, ys, gates):
    T = x2.shape[0]
    tm = TILE
    row = lambda w: pl.BlockSpec((tm, w), lambda i: (i, 0))
    return pl.pallas_call(
        _combine_kernel,
        grid=(T // tm,),
        in_specs=[row(D_MODEL)] * 5 + [row(TOP_K)],
        out_specs=row(D_MODEL),
        out_shape=jax.ShapeDtypeStruct((T, D_MODEL), F32),
        compiler_params=_params("parallel"),
        name="combine",
    )(x2, *ys, gates)


def _visit_schedule(counts, nblk, bm):
    ends = jnp.cumsum(counts)
    starts = ends - counts
    first_blk = starts // bm
    nvis = jnp.where(counts > 0, (ends - 1) // bm - first_blk + 1, 0)
    vis_end = jnp.cumsum(nvis)
    vis_start = vis_end - nvis
    total = vis_end[-1]
    v = jnp.arange(nblk + N_EXPERTS - 1, dtype=jnp.int32)
    e = jnp.sum((vis_end[None, :] <= v[:, None]).astype(jnp.int32), axis=1)
    e = jnp.clip(e, 0, N_EXPERTS - 1)
    valid = v < total
    e_last = jnp.max(jnp.where(counts > 0, jnp.arange(N_EXPERTS, dtype=jnp.int32), 0))
    e = jnp.where(valid, e, e_last)
    blk = jnp.where(valid, first_blk[e] + v - vis_start[e], nblk - 1)
    lo = jnp.where(valid, starts[e], 0)
    hi = jnp.where(valid, ends[e], 0)
    i32 = lambda a: a.astype(jnp.int32)
    return i32(blk), i32(e), i32(lo), i32(hi)


def _moe(x2, h, top_idx, gates, rank, counts, w_up, b_up, w_down, b_down, layer):
    T = x2.shape[0]
    A = T * TOP_K
    bm = EXPERT_ROWS
    assert A % bm == 0
    counts = counts[0]
    starts = jnp.cumsum(counts) - counts
    experts = jnp.arange(N_EXPERTS, dtype=jnp.int32)
    start_of = jnp.sum(jnp.where(top_idx[..., None] == experts, starts, 0), axis=-1)
    inv4 = start_of + rank
    iota = jnp.arange(A, dtype=jnp.int32)
    _, order = lax.sort((inv4.reshape(-1), iota), num_keys=1)
    x_sorted = jnp.take(h, order // TOP_K, axis=0, mode='clip')
    meta = _visit_schedule(counts, A // bm, bm)
    y_sorted = _experts(meta, x_sorted, w_up, b_up, w_down, b_down, layer)
    ys = [jnp.take(y_sorted, inv4[:, k], axis=0, mode='clip') for k in range(TOP_K)]
    return _combine(x2, ys, gates)


def _rotary_tables(seq):
    half = ROT_DIM // 2
    inv_freq = ROPE_THETA ** (-jnp.arange(0, ROT_DIM, 2, dtype=F32) / ROT_DIM)
    ang = jnp.arange(seq).astype(F32)[:, None] * inv_freq[None, :]
    cos, sin = jnp.cos(ang), jnp.sin(ang)
    rest = ATTN_HEAD_DIM - ROT_DIM
    ones = jnp.ones((seq, rest), F32)
    zeros = jnp.zeros((seq, rest), F32)
    zh = jnp.zeros((seq, half), F32)
    cos_a = jnp.concatenate([cos, cos, ones], axis=1)
    sin_b = jnp.concatenate([-sin, zh, zeros], axis=1)
    sin_c = jnp.concatenate([zh, sin, zeros], axis=1)
    tile = lambda a: jnp.tile(a, (1, LANES // ATTN_HEAD_DIM))
    return tile(cos_a), tile(sin_b), tile(sin_c)


def _constants():
    idx = jnp.arange(ATTN_WIDTH) // ATTN_HEAD_DIM
    bd = (idx[:, None] == idx[None, :]).astype(BF16)

    def perm(d):
        new = jnp.arange(TILE)
        old = (new % (TILE // d)) * d + new // (TILE // d)
        return (old[:, None] == jnp.arange(TILE)[None, :]).astype(BF16)

    p4, p16 = perm(4), perm(16)
    expand = (jnp.arange(LANES)[:, None] == idx[None, :]).astype(BF16)
    tri = (jnp.arange(TILE)[:, None] >= jnp.arange(TILE)[None, :]).astype(BF16)
    return dict(bd=bd, p4=p4, p16=p16, p4t=p4.T, p16t=p16.T, expand=expand, tri=tri)


def _layer_weights(l, norm1_g, w_in, q_norm_g, k_norm_g, attn_out_g, conv_w, conv_b, igate_b, fgate_b,
                   mlstm_out_g, w_out, norm2_g, router_w, router_b):
    aw, mw, nh = ATTN_WIDTH, MLSTM_WIDTH, MLSTM_HEADS
    n_main = 3 * aw + 4 * mw
    w = w_in[l]
    wg = w[:, n_main:]
    pick = lambda a, d: jnp.concatenate([a[..., d * nh:(d + 1) * nh],
                                         a[..., (2 + d) * nh:(3 + d) * nh]], axis=-1)
    gate_bias = [jnp.concatenate([igate_b[l, d], fgate_b[l, d]]) for d in range(2)]
    return dict(
        g1=norm1_g[l][None, :], w_main=w[:, :n_main].astype(BF16),
        w_gate=jnp.concatenate([pick(wg, 0), pick(wg, 1)], axis=1),
        qg=jnp.tile(q_norm_g[l], ATTN_HEADS)[None, :] * (ATTN_HEAD_DIM ** -0.5),
        kg=jnp.tile(k_norm_g[l], ATTN_HEADS)[None, :],
        ag=attn_out_g[l][None, :], cw=conv_w[l], cb=conv_b[l][None, :],
        bias_c=[b[None, :] for b in gate_bias], bias_r=[b[:, None] for b in gate_bias],
        og=mlstm_out_g[l][None, :], w_out=w_out[l].astype(BF16), g2=norm2_g[l][None, :],
        rw=router_w[l], rb=router_b[l][None, :])


def kernel(x_prompt, x_sample, norm1_g, w_in, q_norm_g, k_norm_g, attn_out_g, conv_w, conv_b, igate_b,
           fgate_b, mlstm_out_g, w_out, norm2_g, router_w, router_b, w_up, b_up, w_down, b_down):
    assert all(w // (2 * d) == HALO for w, d in SEGMENTS)
    groups = (x_prompt.shape[:2], x_sample.shape[:2])
    span = max(d for _, d in SEGMENTS) * LQ
    assert all(s % span == 0 for _, s in groups), groups
    small = (norm1_g, w_in, q_norm_g, k_norm_g, attn_out_g, conv_w, conv_b, igate_b, fgate_b,
             mlstm_out_g, w_out, norm2_g, router_w, router_b)
    consts = _constants()
    tables = _rotary_tables(max(s for _, s in groups))
    ngate = 2 * MLSTM_HEADS

    def layer(x2, lw, l, grp):
        chunk_rows = lambda g: jnp.swapaxes(g.reshape(x2.shape[0] // CHUNK, CHUNK, ngate), 1, 2)
        (q1, k1, v1, q4, k4, v4, q16, k16, v16, qkm, vm, om, gf, gb) = _inproj(x2, lw, tables, consts, grp)
        segs = [_attn_segment(q1, k1, v1, 1, grp), _attn_segment(q4, k4, v4, 4, grp),
                _attn_segment(q16, k16, v16, 16, grp)]
        hf, qm, km = _mlstm_pass(False, True, (qkm, lw['cw'], lw['cb'], vm, gf, chunk_rows(gf),
                                               lw['bias_c'][0], lw['bias_r'][0]), grp)
        ml = _mlstm_pass(True, False, (qm, km, vm, gb, chunk_rows(gb), lw['bias_c'][1], lw['bias_r'][1],
                                       hf, om, lw['og']), grp)
        x2, h, top_idx, gates, rank, counts = _outproj([s[0] for s in segs], [s[1] for s in segs], ml, x2,
                                                        lw, consts)
        return _moe(x2, h, top_idx, gates, rank, counts, w_up, b_up, w_down, b_down, l)

    (b1, s1), (b2, s2) = groups
    streams = [(x_prompt.reshape(-1, D_MODEL), ((b1, s1), (0, s1))),
               (x_sample.reshape(-1, D_MODEL), ((b2, s2), (0, s2)))]
    for l in range(DEPTH):
        lw = _layer_weights(l, *small)
        streams = [(layer(x2, lw, l, grp), grp) for x2, grp in streams]
    (y_prompt, _), (y_sample, _) = streams
    return (y_prompt.reshape(x_prompt.shape), y_sample.reshape(x_sample.shape))
```

```python
import functools

import jax
import jax.numpy as jnp
from jax import lax
from jax.experimental import pallas as pl
from jax.experimental.pallas import tpu as pltpu

D_MODEL = 1024
DEPTH = 2
ATTN_WIDTH = 512
ATTN_HEAD_DIM = 64
ATTN_HEADS = 8
ROT_DIM = 16
ROPE_THETA = 500000.0
SEGMENTS = ((128, 1), (512, 4), (2048, 16))
HALO = 64
MLSTM_WIDTH = 512
MLSTM_HEAD_DIM = 128
MLSTM_HEADS = 4
CONV_WIDTH = 5
CHUNK = 64
N_EXPERTS = 32
TOP_K = 4
D_FF = 1024
SWIGLU_LIMIT = 7.0
SWIGLU_ALPHA = 1.702
RMS_EPS = 1e-6
NEG_INF = -1e30

LANES = 128
SUBLANES = 8
TILE = 512
LQ = 128
EXPERT_ROWS = 512
VMEM_LIMIT = 48 * 1024 * 1024
VMEM_LIMIT_EXPERTS = 56 * 1024 * 1024

F32 = jnp.float32
BF16 = jnp.bfloat16
NT_DIMS = (((1,), (1,)), ((), ()))
TN_DIMS = (((0,), (0,)), ((), ()))


def _params(*sem, vmem=VMEM_LIMIT):
    return pltpu.CompilerParams(dimension_semantics=sem, vmem_limit_bytes=vmem)


def _seq_bounds(row, groups):
    (b1, s1), (_, s2) = groups
    t1 = b1 * s1
    in1 = row < t1
    start = jnp.where(in1, (row // s1) * s1, t1 + ((row - t1) // s2) * s2)
    return start, start + jnp.where(in1, s1, s2)


def _split3(a):
    hi = a.astype(BF16)
    r1 = a - hi.astype(F32)
    mid = r1.astype(BF16)
    lo = (r1 - mid.astype(F32)).astype(BF16)
    return hi, mid, lo


def _dot_split(a, b, pieces):
    parts = _split3(a)[:pieces]
    out = jnp.dot(parts[0], b, preferred_element_type=F32)
    for p in parts[1:]:
        out = out + jnp.dot(p, b, preferred_element_type=F32)
    return out


def _dot_split_rhs(a, b, pieces):
    parts = _split3(b)[:pieces]
    out = jnp.dot(a, parts[0], preferred_element_type=F32)
    for p in parts[1:]:
        out = out + jnp.dot(a, p, preferred_element_type=F32)
    return out


def _dot_f32(a, b):
    ah, al, _ = _split3(a)
    bh, bl, _ = _split3(b)
    out = jnp.dot(ah, bh, preferred_element_type=F32)
    out = out + jnp.dot(ah, bl, preferred_element_type=F32)
    return out + jnp.dot(al, bh, preferred_element_type=F32)


def _group_sumsq(z, bd_ref):
    return _dot_split(z * z, bd_ref[...], 1)


def _inproj_kernel(x_ref, g1_ref, w_ref, wg_ref, qg_ref, kg_ref, cos_ref, sinb_ref, sinc_ref,
                   bd_ref, p4_ref, p16_ref,
                   q1_ref, k1_ref, v1_ref, q4_ref, k4_ref, v4_ref, q16_ref, k16_ref, v16_ref,
                   qkm_ref, vm_ref, om_ref, gf_ref, gb_ref):
    aw, mw = ATTN_WIDTH, MLSTM_WIDTH
    sub = x_ref.shape[0] // 2

    def half_tile(i, carry):
        r = pl.ds(pl.multiple_of(i * sub, sub), sub)
        x = x_ref[r, :]
        xn = x * lax.rsqrt(jnp.mean(x * x, axis=-1, keepdims=True) + RMS_EPS) * g1_ref[...]
        xb = xn.astype(BF16)

        def proj(lo, hi):
            return jnp.dot(xb, w_ref[:, lo:hi], preferred_element_type=F32)

        reps = aw // LANES
        cos_a = jnp.concatenate([cos_ref[r, :]] * reps, axis=1)
        sin_b = jnp.concatenate([sinb_ref[r, :]] * reps, axis=1)
        sin_c = jnp.concatenate([sinc_ref[r, :]] * reps, axis=1)

        def norm_rot(z, g_ref):
            ss = _group_sumsq(z, bd_ref)
            y = z * lax.rsqrt(ss * (1.0 / ATTN_HEAD_DIM) + RMS_EPS) * g_ref[...]
            half = ROT_DIM // 2
            y_up = pltpu.roll(y, aw - half, axis=1)
            y_dn = pltpu.roll(y, half, axis=1)
            return (y * cos_a + y_up * sin_b + y_dn * sin_c).astype(BF16)

        q1_ref[r, :] = norm_rot(proj(0, aw), qg_ref)
        k1_ref[r, :] = norm_rot(proj(aw, 2 * aw), kg_ref)
        v1_ref[r, :] = proj(2 * aw, 3 * aw).astype(BF16)
        qkm_ref[r, :] = proj(3 * aw, 3 * aw + 2 * mw)
        vm_ref[r, :] = proj(3 * aw + 2 * mw, 3 * aw + 3 * mw).astype(BF16)
        om_ref[r, :] = proj(3 * aw + 3 * mw, 3 * aw + 4 * mw)
        gates = _dot_f32(xn, wg_ref[...])
        ng = 2 * MLSTM_HEADS
        gf_ref[r, :] = gates[:, :ng]
        gb_ref[r, :] = gates[:, ng:]
        return carry

    lax.fori_loop(0, 2, half_tile, 0)
    for nat_ref, r4_ref, r16_ref in ((q1_ref, q4_ref, q16_ref), (k1_ref, k4_ref, k16_ref),
                                     (v1_ref, v4_ref, v16_ref)):
        z = nat_ref[...]
        r4_ref[...] = jnp.dot(p4_ref[...], z, preferred_element_type=F32).astype(BF16)
        r16_ref[...] = jnp.dot(p16_ref[...], z, preferred_element_type=F32).astype(BF16)


def _inproj(x2, lw, tables, consts, groups):
    T = x2.shape[0]
    tm = TILE
    aw, mw, ng = ATTN_WIDTH, MLSTM_WIDTH, 2 * MLSTM_HEADS
    row = lambda w: pl.BlockSpec((tm, w), lambda i: (i, 0))
    const = lambda a: pl.BlockSpec(a.shape, lambda i: (0, 0))

    def tab_index(i):
        start, _ = _seq_bounds(i * tm, groups)
        return (i - start // tm, 0)

    tab = pl.BlockSpec((tm, LANES), tab_index)
    weights = (lw['g1'], lw['w_main'], lw['w_gate'], lw['qg'], lw['kg'])
    perms = (consts['bd'], consts['p4'], consts['p16'])
    bf = lambda w: jax.ShapeDtypeStruct((T, w), BF16)
    f32 = lambda w: jax.ShapeDtypeStruct((T, w), F32)
    return pl.pallas_call(
        _inproj_kernel,
        grid=(T // tm,),
        in_specs=[row(D_MODEL)] + [const(a) for a in weights] + [tab, tab, tab] + [const(a) for a in perms],
        out_specs=[row(aw)] * 9 + [row(2 * mw), row(mw), row(mw), row(ng), row(ng)],
        out_shape=[bf(aw)] * 9 + [f32(2 * mw), bf(mw), f32(mw), f32(ng), f32(ng)],
        compiler_params=_params("parallel"),
        name="inproj",
    )(x2, *weights, *tables, *perms)


def _attn_kernel(q_ref, kl_ref, kc_ref, kr_ref, vl_ref, vc_ref, vr_ref, o_ref, lse_ref, *,
                 dilation, groups):
    aw = ATTN_WIDTH
    lq, lk = LQ, LQ + 2 * HALO
    row0 = pl.program_id(0) * (LQ * dilation)
    start, end = _seq_bounds(row0, groups)
    m0 = (row0 - start) // dilation
    sub_len = (end - start) // dilation

    flat = lambda ref, n: ref[...].reshape(n, ref.shape[-1])
    q = flat(q_ref, lq)
    k = jnp.concatenate([flat(kl_ref, HALO), flat(kc_ref, lq), flat(kr_ref, HALO)], axis=0)
    v = jnp.concatenate([flat(vl_ref, HALO), flat(vc_ref, lq), flat(vr_ref, HALO)], axis=0)

    row = lax.broadcasted_iota(jnp.int32, (2 * lq, lk), 0)
    col = lax.broadcasted_iota(jnp.int32, (2 * lq, lk), 1)
    qrow = jnp.where(row >= lq, row - lq, row)
    rel = col - HALO - qrow
    kpos = m0 - HALO + col
    mask = (jnp.abs(rel) <= HALO) & (kpos >= 0) & (kpos < sub_len)

    lane2 = lax.broadcasted_iota(jnp.int32, (2 * lq, LANES), 1)
    row2 = lax.broadcasted_iota(jnp.int32, (2 * lq, LANES), 0)
    own = (lane2 >= ATTN_HEAD_DIM) ^ (row2 < lq)
    lane = lax.broadcasted_iota(jnp.int32, (lq, LANES), 1)
    first = lane < ATTN_HEAD_DIM

    lse_tile = jnp.zeros((lq, LANES), F32)
    outs = []
    for hp in range(aw // LANES):
        sl = slice(hp * LANES, (hp + 1) * LANES)
        qp = q[:, sl]
        q2 = jnp.concatenate([qp, qp], axis=0)
        q2 = jnp.where(own, q2, jnp.zeros_like(q2))
        s = lax.dot_general(q2, k[:, sl], NT_DIMS, preferred_element_type=F32)
        s = jnp.where(mask, s, NEG_INF)
        mx = jnp.max(s, axis=-1, keepdims=True)
        p = jnp.exp(s - mx)
        den = jnp.sum(p, axis=-1, keepdims=True)
        pv = jnp.dot(p.astype(BF16), v[:, sl], preferred_element_type=F32)
        o2 = pv / den
        lse2 = mx + jnp.log(den)
        outs.append(jnp.where(first, o2[:lq], o2[lq:]).astype(BF16))
        lse_tile = jnp.where(lane == 2 * hp, lse2[:lq], lse_tile)
        lse_tile = jnp.where(lane == 2 * hp + 1, lse2[lq:], lse_tile)
    o_ref[...] = jnp.concatenate(outs, axis=1).reshape(o_ref.shape)
    lse_ref[...] = lse_tile.reshape(lse_ref.shape)


def _attn_segment(q, k, v, dilation, groups):
    T, aw = q.shape
    tm = TILE
    per_tile = tm // dilation
    ntile = T // tm
    if per_tile >= LQ:
        nblk = per_tile // LQ
        nhalo = per_tile // HALO
        hb = LQ // HALO
        shape = lambda w: (ntile * dilation, per_tile, w)
        grid = (ntile * nblk, dilation)
        slab = lambda i, r: (i // nblk) * dilation + r

        def cen(w):
            return pl.BlockSpec((1, LQ, w), lambda i, r: (slab(i, r), i % nblk, 0))

        def halo_spec(side):
            def index(i, r):
                j = (i % nblk) * hb + (-1 if side < 0 else hb)
                t = i // nblk + jnp.where(j < 0, -1, 0) + jnp.where(j >= nhalo, 1, 0)
                t = jnp.clip(t, 0, ntile - 1)
                return (t * dilation + r, j % nhalo, 0)
            return pl.BlockSpec((1, HALO, aw), index)
    else:
        tq = LQ // per_tile
        th = HALO // per_tile
        nhb = ntile // th
        shape = lambda w: (ntile, dilation, per_tile, w)
        grid = (ntile // tq, dilation)

        def cen(w):
            return pl.BlockSpec((tq, None, per_tile, w), lambda i, r: (i, r, 0, 0))

        def halo_spec(side):
            def index(i, r):
                j = i * (tq // th) + (-1 if side < 0 else tq // th)
                return (jnp.clip(j, 0, nhb - 1), r, 0, 0)
            return pl.BlockSpec((th, None, per_tile, aw), index)

    left, right = halo_spec(-1), halo_spec(1)
    view = lambda a: a.reshape(shape(a.shape[-1]))
    o, lse = pl.pallas_call(
        functools.partial(_attn_kernel, dilation=dilation, groups=groups),
        grid=grid,
        in_specs=[cen(aw), left, cen(aw), right, left, cen(aw), right],
        out_specs=[cen(aw), cen(LANES)],
        out_shape=[jax.ShapeDtypeStruct(shape(aw), BF16), jax.ShapeDtypeStruct(shape(LANES), F32)],
        compiler_params=_params("parallel", "parallel"),
        name=f"attn_d{dilation}",
    )(view(q), view(k), view(k), view(k), view(v), view(v), view(v))
    return o.reshape(T, aw), lse.reshape(T, LANES)


def _log_sigmoid(x):
    return jnp.minimum(x, 0.0) - jnp.log(1.0 + jnp.exp(-jnp.abs(x)))


def _mlstm_kernel(*refs, tb, reverse, first_pass, groups):
    nh, hd = MLSTM_HEADS, MLSTM_HEAD_DIM
    mw = MLSTM_WIDTH
    if first_pass:
        (qk_ref, qkp_ref, qkn_ref, cw_ref, cb_ref, v_ref, gc_ref, gr_ref, bc_ref, br_ref,
         h_ref, qo_ref, ko_ref, c_scr, m_scr, intra_scr, upd_scr, col_scr, one_scr,
         q_scr, k_scr, xe_scr) = refs
    else:
        (q_ref, k_ref, v_ref, gc_ref, gr_ref, bc_ref, br_ref, hf_ref, om_ref, og_ref,
         h_ref, c_scr, m_scr, intra_scr, upd_scr, col_scr, one_scr) = refs
    step = pl.program_id(0)
    blk = pl.num_programs(0) - 1 - step if reverse else step
    row0 = blk * tb
    start, end = _seq_bounds(row0, groups)
    at_start = row0 == start
    at_end = row0 + tb == end

    @pl.when(at_end if reverse else at_start)
    def _():
        c_scr[...] = jnp.zeros_like(c_scr)
        m_scr[...] = jnp.zeros_like(m_scr)

    if first_pass:
        pad = CONV_WIDTH // 2
        xe_scr[0:SUBLANES, :] = jnp.where(at_start, 0.0, qkp_ref[...])
        xe_scr[SUBLANES:SUBLANES + tb, :] = qk_ref[...]
        xe_scr[SUBLANES + tb:, :] = jnp.where(at_end, 0.0, qkn_ref[...])
        acc = jnp.zeros((tb, 2 * mw), F32) + cb_ref[...]
        for j in range(CONV_WIDTH):
            acc = acc + xe_scr[SUBLANES - pad + j:SUBLANES - pad + j + tb, :] * cw_ref[j:j + 1, :]
        act = acc * jax.nn.sigmoid(acc)
        qb = act[:, :mw].astype(BF16)
        kb = (act[:, mw:] * (hd ** -0.5)).astype(BF16)
        qo_ref[...] = qb
        ko_ref[...] = kb
        q_scr[...] = qb
        k_scr[...] = kb
        q_src, k_src = q_scr, k_scr
    else:
        q_src, k_src = q_ref, k_ref

    ti = lax.broadcasted_iota(jnp.int32, (CHUNK, CHUNK), 0)
    si = lax.broadcasted_iota(jnp.int32, (CHUNK, CHUNK), 1)
    causal = (si >= ti) if reverse else (si <= ti)
    cum_col = causal.astype(BF16)
    cum_row = ((ti >= si) if reverse else (ti <= si)).astype(BF16)
    last = 0 if reverse else CHUNK - 1
    ones_col = (lax.broadcasted_iota(jnp.int32, (CHUNK, hd), 1) == 0).astype(BF16)
    ngate = 2 * nh
    gc_is_i = lax.broadcasted_iota(jnp.int32, (CHUNK, ngate), 1) < nh
    gr_is_i = lax.broadcasted_iota(jnp.int32, (ngate, CHUNK), 0) < nh
    nchunk = tb // CHUNK

    def local_terms(j, carry):
        rows = pl.ds(j * CHUNK, CHUNK)
        a_c = gc_ref[rows, :] + bc_ref[...]
        a_r = gr_ref[j] + br_ref[...]
        gl_c = jnp.where(gc_is_i, a_c, _log_sigmoid(a_c))
        gl_r = jnp.where(gr_is_i, a_r, _log_sigmoid(a_r))
        b_c = _dot_split_rhs(cum_col, gl_c, 3)
        b_r = _dot_split(gl_r, cum_row, 3)
        qc = q_src[rows, :]
        kc = k_src[rows, :]
        vc = v_ref[rows, :]
        for h in range(nh):
            hs = slice(h * hd, (h + 1) * hd)
            bc = b_c[:, nh + h:nh + h + 1]
            br = b_r[nh + h:nh + h + 1, :]
            lir = gl_r[h:h + 1, :]
            lic = gl_c[:, h:h + 1]
            dmat = jnp.where(causal, bc - br + lir, NEG_INF)
            m_loc = jnp.max(dmat, axis=-1, keepdims=True)
            qh, kh, vh = qc[:, hs], kc[:, hs], vc[:, hs]
            s = lax.dot_general(qh, kh, NT_DIMS, preferred_element_type=F32) * jnp.exp(dmat - m_loc)
            v_aug = jnp.concatenate([vh, ones_col], axis=1)
            intra_scr[j, h] = jnp.dot(s.astype(BF16), v_aug, preferred_element_type=F32)
            b_last = bc[last:last + 1, :]
            log_w = b_last - bc + lic
            mw_loc = jnp.max(log_w, axis=0, keepdims=True)
            wv = (jnp.exp(log_w - mw_loc) * v_aug.astype(F32)).astype(BF16)
            upd_scr[j, h] = lax.dot_general(kh, wv, TN_DIMS, preferred_element_type=F32)
            col_scr[j, h, :, 0:1] = bc
            col_scr[j, h, :, 1:2] = m_loc
            one_scr[j, h, :, 0:1] = b_last
            one_scr[j, h, :, 1:2] = mw_loc
        return carry

    for j in range(nchunk):
        local_terms(j, 0)

    def recurrence(j, carry):
        jj = nchunk - 1 - j if reverse else j
        r0 = pl.multiple_of(jj * CHUNK, CHUNK)
        rows = pl.ds(r0, CHUNK)
        qc = q_src[rows, :]
        if not first_pass:
            hfc = hf_ref[rows, :]
            omc = om_ref[rows, :]
        for h in range(nh):
            hs = slice(h * hd, (h + 1) * hd)
            m_prev = m_scr[h:h + 1, 0:1]
            bc = col_scr[jj, h, :, 0:1]
            m_loc = col_scr[jj, h, :, 1:2]
            b_last = one_scr[jj, h, :, 0:1]
            mw_loc = one_scr[jj, h, :, 1:2]
            inter = bc + m_prev
            m_t = jnp.maximum(inter, m_loc)
            c_old = c_scr[h]
            h_aug = (jnp.exp(m_loc - m_t) * intra_scr[jj, h]
                     + jnp.exp(inter - m_t) * jnp.dot(qc[:, hs], c_old.astype(BF16), preferred_element_type=F32))
            num = h_aug[:, :hd]
            den = h_aug[:, hd:hd + 1]
            h_out = num / jnp.maximum(jnp.abs(den), jnp.exp(-m_t))
            m_new = jnp.maximum(b_last + m_prev, mw_loc)
            c_scr[h] = jnp.exp(b_last + m_prev - m_new) * c_old + jnp.exp(mw_loc - m_new) * upd_scr[jj, h]
            m_scr[h:h + 1, :] = jnp.broadcast_to(m_new, (1, LANES))
            if first_pass:
                h_ref[rows, hs] = h_out
            else:
                y = jax.nn.sigmoid(omc[:, hs]) * (hfc[:, hs] + h_out)
                y = y * lax.rsqrt(jnp.mean(y * y, axis=-1, keepdims=True) + RMS_EPS) * og_ref[:, hs]
                h_ref[rows, hs] = y
        return carry

    lax.fori_loop(0, nchunk, recurrence, 0)


def _mlstm_pass(reverse, first_pass, operands, groups):
    mw, nh, hd = MLSTM_WIDTH, MLSTM_HEADS, MLSTM_HEAD_DIM
    tb = TILE
    T = operands[0].shape[0]
    nblk = T // tb
    ngate = 2 * nh
    bidx = (lambda s: nblk - 1 - s) if reverse else (lambda s: s)
    rows = lambda w: pl.BlockSpec((tb, w), lambda s: (bidx(s), 0))
    const = lambda a: pl.BlockSpec(a.shape, lambda s: (0,) * a.ndim)
    gate_rows = pl.BlockSpec((tb // CHUNK, ngate, CHUNK), lambda s: (bidx(s), 0, 0))
    hb = tb // SUBLANES
    nhb = T // SUBLANES
    nchunk = tb // CHUNK
    state = [pltpu.VMEM((nh, hd, 2 * hd), F32), pltpu.VMEM((SUBLANES, LANES), F32),
             pltpu.VMEM((nchunk, nh, CHUNK, 2 * hd), F32), pltpu.VMEM((nchunk, nh, hd, 2 * hd), F32),
             pltpu.VMEM((nchunk, nh, CHUNK, 2), F32), pltpu.VMEM((nchunk, nh, 1, 2), F32)]
    if first_pass:
        qk, cw, cb, v, gc, gr, bc, br = operands
        prev = pl.BlockSpec((SUBLANES, 2 * mw), lambda s: (jnp.maximum(bidx(s) * hb - 1, 0), 0))
        nxt = pl.BlockSpec((SUBLANES, 2 * mw), lambda s: (jnp.minimum((bidx(s) + 1) * hb, nhb - 1), 0))
        in_specs = [rows(2 * mw), prev, nxt, const(cw), const(cb), rows(mw), rows(ngate), gate_rows,
                    const(bc), const(br)]
        args = (qk, qk, qk, cw, cb, v, gc, gr, bc, br)
        out_specs = [rows(mw), rows(mw), rows(mw)]
        out_shape = [jax.ShapeDtypeStruct((T, mw), F32), jax.ShapeDtypeStruct((T, mw), BF16),
                     jax.ShapeDtypeStruct((T, mw), BF16)]
        scratch = state + [pltpu.VMEM((tb, mw), BF16), pltpu.VMEM((tb, mw), BF16),
                           pltpu.VMEM((tb + 2 * SUBLANES, 2 * mw), F32)]
    else:
        q, k, v, gc, gr, bc, br, hf, om, og = operands
        in_specs = [rows(mw), rows(mw), rows(mw), rows(ngate), gate_rows, const(bc), const(br),
                    rows(mw), rows(mw), const(og)]
        args = operands
        out_specs = rows(mw)
        out_shape = jax.ShapeDtypeStruct((T, mw), F32)
        scratch = state
    return pl.pallas_call(
        functools.partial(_mlstm_kernel, tb=tb, reverse=reverse, first_pass=first_pass, groups=groups),
        grid=(nblk,),
        in_specs=in_specs,
        out_specs=out_specs,
        out_shape=out_shape,
        scratch_shapes=scratch,
        compiler_params=_params("arbitrary"),
        name="mlstm_fwd" if first_pass else "mlstm_bwd",
    )(*args)


def _outproj_kernel(o1_ref, o4_ref, o16_ref, l1_ref, l4_ref, l16_ref, ml_ref, x_ref, ag_ref, bd_ref,
                    p4t_ref, p16t_ref, ex_ref, w_ref, g2_ref, rw_ref, rb_ref, tri_ref,
                    y_ref, h_ref, idx_ref, gate_ref, rank_ref, cnt_ref, base_scr):
    unperm = lambda p_ref, o_ref: jnp.dot(p_ref[...], o_ref[...], preferred_element_type=F32)
    o1 = o1_ref[...].astype(F32)
    o4 = unperm(p4t_ref, o4_ref)
    o16 = unperm(p16t_ref, o16_ref)

    def unperm_f32(p_ref, l_ref):
        return _dot_split_rhs(p_ref[...], l_ref[...], 2)

    l1 = l1_ref[...]
    l4 = unperm_f32(p4t_ref, l4_ref)
    l16 = unperm_f32(p16t_ref, l16_ref)
    mx = jnp.maximum(jnp.maximum(l1, l4), l16)
    e1, e4, e16 = jnp.exp(l1 - mx), jnp.exp(l4 - mx), jnp.exp(l16 - mx)
    inv = 1.0 / (e1 + e4 + e16)
    expand = lambda w: _dot_split(w, ex_ref[...], 2)
    attn = o1 + expand(e4 * inv) * (o4 - o1) + expand(e16 * inv) * (o16 - o1)
    ss = _group_sumsq(attn, bd_ref)
    attn = attn * lax.rsqrt(ss * (1.0 / ATTN_HEAD_DIM) + RMS_EPS) * ag_ref[...]
    aw = ATTN_WIDTH
    y = jnp.dot(attn.astype(BF16), w_ref[:aw, :], preferred_element_type=F32)
    y = y + jnp.dot(ml_ref[...].astype(BF16), w_ref[aw:, :], preferred_element_type=F32)
    x = x_ref[...] + y
    y_ref[...] = x
    h = x * lax.rsqrt(jnp.mean(x * x, axis=-1, keepdims=True) + RMS_EPS) * g2_ref[...]
    h_ref[...] = h.astype(BF16)
    logits = _dot_f32(h, rw_ref[...]) + rb_ref[...]

    @pl.when(pl.program_id(0) == 0)
    def _():
        base_scr[...] = jnp.zeros_like(base_scr)

    tm = logits.shape[0]
    lane = lax.broadcasted_iota(jnp.int32, (tm, N_EXPERTS), 1).astype(F32)
    slot = lax.broadcasted_iota(jnp.int32, (tm, TOP_K), 1)
    work = logits
    vals, sels = [], []
    idx4 = jnp.zeros((tm, TOP_K), F32)
    for r in range(TOP_K):
        mx_r = jnp.max(work, axis=-1, keepdims=True)
        idx_r = jnp.min(jnp.where(work == mx_r, lane, float(N_EXPERTS)), axis=-1, keepdims=True)
        sel = lane == idx_r
        work = jnp.where(sel, -jnp.inf, work)
        vals.append(mx_r)
        sels.append(sel)
        idx4 = jnp.where(slot == r, idx_r, idx4)
    exps = [jnp.exp(v - vals[0]) for v in vals]
    inv_den = 1.0 / sum(exps)
    gate4 = jnp.zeros((tm, TOP_K), F32)
    for r in range(TOP_K):
        gate4 = jnp.where(slot == r, exps[r] * inv_den, gate4)
    onehot = sum(s.astype(F32) for s in sels)
    incl = jnp.dot(tri_ref[...], onehot.astype(BF16), preferred_element_type=F32)
    before = base_scr[0:1, 0:N_EXPERTS] + incl - onehot
    rank4 = jnp.zeros((tm, TOP_K), F32)
    for r in range(TOP_K):
        rank_r = jnp.sum(jnp.where(sels[r], before, 0.0), axis=-1, keepdims=True)
        rank4 = jnp.where(slot == r, rank_r, rank4)
    total = base_scr[0:1, 0:N_EXPERTS] + incl[tm - 1:tm, :]
    base_scr[0:1, 0:N_EXPERTS] = total
    idx_ref[...] = idx4.astype(jnp.int32)
    gate_ref[...] = gate4
    rank_ref[...] = rank4.astype(jnp.int32)
    cnt_ref[...] = total.astype(jnp.int32)


def _outproj(os_, ls_, ml, x2, lw, consts):
    T = x2.shape[0]
    tm = TILE
    row = lambda w: pl.BlockSpec((tm, w), lambda i: (i, 0))
    const = lambda a: pl.BlockSpec(a.shape, lambda i: (0, 0))
    aw = ATTN_WIDTH
    cs = (lw['ag'], consts['bd'], consts['p4t'], consts['p16t'], consts['expand'], lw['w_out'],
          lw['g2'], lw['rw'], lw['rb'], consts['tri'])
    i32 = jnp.int32
    return pl.pallas_call(
        _outproj_kernel,
        grid=(T // tm,),
        in_specs=[row(aw)] * 3 + [row(LANES)] * 3 + [row(MLSTM_WIDTH), row(D_MODEL)] + [const(a) for a in cs],
        out_specs=[row(D_MODEL), row(D_MODEL), row(TOP_K), row(TOP_K), row(TOP_K),
                   pl.BlockSpec((1, N_EXPERTS), lambda i: (0, 0))],
        out_shape=[jax.ShapeDtypeStruct((T, D_MODEL), F32), jax.ShapeDtypeStruct((T, D_MODEL), BF16),
                   jax.ShapeDtypeStruct((T, TOP_K), i32), jax.ShapeDtypeStruct((T, TOP_K), F32),
                   jax.ShapeDtypeStruct((T, TOP_K), i32), jax.ShapeDtypeStruct((1, N_EXPERTS), i32)],
        scratch_shapes=[pltpu.VMEM((SUBLANES, LANES), F32)],
        compiler_params=_params("arbitrary"),
        name="outproj",
    )(*os_, *ls_, ml, x2, *cs)


def _expert_kernel(blk_ref, exp_ref, lo_ref, hi_ref, x_ref, wu_ref, bu_ref, wd_ref, bd_ref, y_ref,
                   wu_scr, wd_scr, *, bm):
    v = pl.program_id(0)
    pv = jnp.maximum(v - 1, 0)
    new_expert = (v == 0) | (exp_ref[v] != exp_ref[pv])
    first_visit = (v == 0) | (blk_ref[v] != blk_ref[pv])
    cast_rows = 128

    @pl.when(new_expert)
    def _():
        def cast(i, c):
            r = pl.ds(pl.multiple_of(i * cast_rows, cast_rows), cast_rows)
            wu_scr[r, :] = wu_ref[0, 0, r, :].astype(BF16)
            wd_scr[r, :] = wd_ref[0, 0, r, :].astype(BF16)
            return c
        lax.fori_loop(0, D_MODEL // cast_rows, cast, 0)

    lo, hi = lo_ref[v], hi_ref[v]

    @pl.when(hi > lo)
    def _():
        x = x_ref[...]
        half = D_FF // 2
        y = jnp.zeros((bm, D_MODEL), F32) + bd_ref[0, 0]
        for c in range(2):
            g0, l0 = c * half, D_FF + c * half
            x_glu = jnp.dot(x, wu_scr[:, g0:g0 + half], preferred_element_type=F32) + bu_ref[0, 0, :, g0:g0 + half]
            x_lin = jnp.dot(x, wu_scr[:, l0:l0 + half], preferred_element_type=F32) + bu_ref[0, 0, :, l0:l0 + half]
            x_glu = jnp.minimum(x_glu, SWIGLU_LIMIT)
            x_lin = jnp.clip(x_lin, -SWIGLU_LIMIT, SWIGLU_LIMIT)
            act = x_glu * jax.nn.sigmoid(SWIGLU_ALPHA * x_glu) * (x_lin + 1.0)
            y = y + jnp.dot(act.astype(BF16), wd_scr[g0:g0 + half, :], preferred_element_type=F32)
        rows = blk_ref[v] * bm + lax.broadcasted_iota(jnp.int32, (bm, 1), 0)
        mine = (rows >= lo) & (rows < hi)

        @pl.when(first_visit)
        def _():
            y_ref[...] = jnp.where(mine, y, 0.0).astype(y_ref.dtype)

        @pl.when(jnp.logical_not(first_visit))
        def _():
            y_ref[...] = jnp.where(mine, y.astype(y_ref.dtype), y_ref[...])


def _experts(meta, x_sorted, w_up, b_up, w_down, b_down, layer):
    A = x_sorted.shape[0]
    bm = EXPERT_ROWS
    nvisit = meta[0].shape[0]
    grid_spec = pltpu.PrefetchScalarGridSpec(
        num_scalar_prefetch=4,
        grid=(nvisit,),
        in_specs=[pl.BlockSpec((bm, D_MODEL), lambda v, blk, ex, lo, hi: (blk[v], 0)),
                  pl.BlockSpec((1, 1, D_MODEL, 2 * D_FF), lambda v, blk, ex, lo, hi: (layer, ex[v], 0, 0)),
                  pl.BlockSpec((1, 1, 1, 2 * D_FF), lambda v, blk, ex, lo, hi: (layer, ex[v], 0, 0)),
                  pl.BlockSpec((1, 1, D_FF, D_MODEL), lambda v, blk, ex, lo, hi: (layer, ex[v], 0, 0)),
                  pl.BlockSpec((1, 1, 1, D_MODEL), lambda v, blk, ex, lo, hi: (layer, ex[v], 0, 0))],
        out_specs=pl.BlockSpec((bm, D_MODEL), lambda v, blk, ex, lo, hi: (blk[v], 0)),
        scratch_shapes=[pltpu.VMEM((D_MODEL, 2 * D_FF), BF16), pltpu.VMEM((D_FF, D_MODEL), BF16)],
    )
    return pl.pallas_call(
        functools.partial(_expert_kernel, bm=bm),
        grid_spec=grid_spec,
        out_shape=jax.ShapeDtypeStruct((A, D_MODEL), BF16),
        compiler_params=_params("arbitrary", vmem=VMEM_LIMIT_EXPERTS),
        name="experts",
    )(*meta, x_sorted, w_up, b_up[:, :, None, :], w_down, b_down[:, :, None, :])


def _combine_kernel(x_ref, y0_ref, y1_ref, y2_ref, y3_ref, g_ref, o_ref):
    acc = x_ref[...]
    g = g_ref[...]
    for k, y_ref in enumerate((y0_ref, y1_ref, y2_ref, y3_ref)):
        acc = acc + y_ref[...].astype(F32) * g[:, k:k + 1]
    o_ref[...] = acc


def _combine(x2, ys, gates):
    T = x2.shape[0]
    tm = TILE
    row = lambda w: pl.BlockSpec((tm, w), lambda i: (i, 0))
    return pl.pallas_call(
        _combine_kernel,
        grid=(T // tm,),
        in_specs=[row(D_MODEL)] * 5 + [row(TOP_K)],
        out_specs=row(D_MODEL),
        out_shape=jax.ShapeDtypeStruct((T, D_MODEL), F32),
        compiler_params=_params("parallel"),
        name="combine",
    )(x2, *ys, gates)


def _visit_schedule(counts, nblk, bm):
    ends = jnp.cumsum(counts)
    starts = ends - counts
    first_blk = starts // bm
    nvis = jnp.where(counts > 0, (ends - 1) // bm - first_blk + 1, 0)
    vis_end = jnp.cumsum(nvis)
    vis_start = vis_end - nvis
    total = vis_end[-1]
    v = jnp.arange(nblk + N_EXPERTS - 1, dtype=jnp.int32)
    e = jnp.sum((vis_end[None, :] <= v[:, None]).astype(jnp.int32), axis=1)
    e = jnp.clip(e, 0, N_EXPERTS - 1)
    valid = v < total
    e_last = jnp.max(jnp.where(counts > 0, jnp.arange(N_EXPERTS, dtype=jnp.int32), 0))
    e = jnp.where(valid, e, e_last)
    blk = jnp.where(valid, first_blk[e] + v - vis_start[e], nblk - 1)
    lo = jnp.where(valid, starts[e], 0)
    hi = jnp.where(valid, ends[e], 0)
    i32 = lambda a: a.astype(jnp.int32)
    return i32(blk), i32(e), i32(lo), i32(hi)


def _moe(x2, h, top_idx, gates, rank, counts, w_up, b_up, w_down, b_down, layer):
    T = x2.shape[0]
    A = T * TOP_K
    bm = EXPERT_ROWS
    assert A % bm == 0
    counts = counts[0]
    starts = jnp.cumsum(counts) - counts
    experts = jnp.arange(N_EXPERTS, dtype=jnp.int32)
    start_of = jnp.sum(jnp.where(top_idx[..., None] == experts, starts, 0), axis=-1)
    inv4 = start_of + rank
    iota = jnp.arange(A, dtype=jnp.int32)
    _, order = lax.sort((inv4.reshape(-1), iota), num_keys=1)
    x_sorted = jnp.take(h, order // TOP_K, axis=0, mode='clip')
    meta = _visit_schedule(counts, A // bm, bm)
    y_sorted = _experts(meta, x_sorted, w_up, b_up, w_down, b_down, layer)
    ys = [jnp.take(y_sorted, inv4[:, k], axis=0, mode='clip') for k in range(TOP_K)]
    return _combine(x2, ys, gates)


def _rotary_tables(seq):
    half = ROT_DIM // 2
    inv_freq = ROPE_THETA ** (-jnp.arange(0, ROT_DIM, 2, dtype=F32) / ROT_DIM)
    ang = jnp.arange(seq).astype(F32)[:, None] * inv_freq[None, :]
    cos, sin = jnp.cos(ang), jnp.sin(ang)
    rest = ATTN_HEAD_DIM - ROT_DIM
    ones = jnp.ones((seq, rest), F32)
    zeros = jnp.zeros((seq, rest), F32)
    zh = jnp.zeros((seq, half), F32)
    cos_a = jnp.concatenate([cos, cos, ones], axis=1)
    sin_b = jnp.concatenate([-sin, zh, zeros], axis=1)
    sin_c = jnp.concatenate([zh, sin, zeros], axis=1)
    tile = lambda a: jnp.tile(a, (1, LANES // ATTN_HEAD_DIM))
    return tile(cos_a), tile(sin_b), tile(sin_c)


def _constants():
    idx = jnp.arange(ATTN_WIDTH) // ATTN_HEAD_DIM
    bd = (idx[:, None] == idx[None, :]).astype(BF16)

    def perm(d):
        new = jnp.arange(TILE)
        old = (new % (TILE // d)) * d + new // (TILE // d)
        return (old[:, None] == jnp.arange(TILE)[None, :]).astype(BF16)

    p4, p16 = perm(4), perm(16)
    expand = (jnp.arange(LANES)[:, None] == idx[None, :]).astype(BF16)
    tri = (jnp.arange(TILE)[:, None] >= jnp.arange(TILE)[None, :]).astype(BF16)
    return dict(bd=bd, p4=p4, p16=p16, p4t=p4.T, p16t=p16.T, expand=expand, tri=tri)


def _layer_weights(l, norm1_g, w_in, q_norm_g, k_norm_g, attn_out_g, conv_w, conv_b, igate_b, fgate_b,
                   mlstm_out_g, w_out, norm2_g, router_w, router_b):
    aw, mw, nh = ATTN_WIDTH, MLSTM_WIDTH, MLSTM_HEADS
    n_main = 3 * aw + 4 * mw
    w = w_in[l]
    wg = w[:, n_main:]
    pick = lambda a, d: jnp.concatenate([a[..., d * nh:(d + 1) * nh],
                                         a[..., (2 + d) * nh:(3 + d) * nh]], axis=-1)
    gate_bias = [jnp.concatenate([igate_b[l, d], fgate_b[l, d]]) for d in range(2)]
    return dict(
        g1=norm1_g[l][None, :], w_main=w[:, :n_main].astype(BF16),
        w_gate=jnp.concatenate([pick(wg, 0), pick(wg, 1)], axis=1),
        qg=jnp.tile(q_norm_g[l], ATTN_HEADS)[None, :] * (ATTN_HEAD_DIM ** -0.5),
        kg=jnp.tile(k_norm_g[l], ATTN_HEADS)[None, :],
        ag=attn_out_g[l][None, :], cw=conv_w[l], cb=conv_b[l][None, :],
        bias_c=[b[None, :] for b in gate_bias], bias_r=[b[:, None] for b in gate_bias],
        og=mlstm_out_g[l][None, :], w_out=w_out[l].astype(BF16), g2=norm2_g[l][None, :],
        rw=router_w[l], rb=router_b[l][None, :])


def kernel(x_prompt, x_sample, norm1_g, w_in, q_norm_g, k_norm_g, attn_out_g, conv_w, conv_b, igate_b,
           fgate_b, mlstm_out_g, w_out, norm2_g, router_w, router_b, w_up, b_up, w_down, b_down):
    assert all(w // (2 * d) == HALO for w, d in SEGMENTS)
    groups = (x_prompt.shape[:2], x_sample.shape[:2])
    span = max(d for _, d in SEGMENTS) * LQ
    assert all(s % span == 0 for _, s in groups), groups
    small = (norm1_g, w_in, q_norm_g, k_norm_g, attn_out_g, conv_w, conv_b, igate_b, fgate_b,
             mlstm_out_g, w_out, norm2_g, router_w, router_b)
    consts = _constants()
    tables = _rotary_tables(max(s for _, s in groups))
    ngate = 2 * MLSTM_HEADS

    def layer(x2, lw, l, grp):
        chunk_rows = lambda g: jnp.swapaxes(g.reshape(x2.shape[0] // CHUNK, CHUNK, ngate), 1, 2)
        (q1, k1, v1, q4, k4, v4, q16, k16, v16, qkm, vm, om, gf, gb) = _inproj(x2, lw, tables, consts, grp)
        segs = [_attn_segment(q1, k1, v1, 1, grp), _attn_segment(q4, k4, v4, 4, grp),
                _attn_segment(q16, k16, v16, 16, grp)]
        hf, qm, km = _mlstm_pass(False, True, (qkm, lw['cw'], lw['cb'], vm, gf, chunk_rows(gf),
                                               lw['bias_c'][0], lw['bias_r'][0]), grp)
        ml = _mlstm_pass(True, False, (qm, km, vm, gb, chunk_rows(gb), lw['bias_c'][1], lw['bias_r'][1],
                                       hf, om, lw['og']), grp)
        x2, h, top_idx, gates, rank, counts = _outproj([s[0] for s in segs], [s[1] for s in segs], ml, x2,
                                                        lw, consts)
        return _moe(x2, h, top_idx, gates, rank, counts, w_up, b_up, w_down, b_down, l)

    (b1, s1), (b2, s2) = groups
    t1 = b1 * s1
    n_a = min(max(round(((t1 + b2 * s2) / 2 - t1) / s2), 0), b2)
    xp = x_prompt.reshape(-1, D_MODEL)
    xs = x_sample.reshape(-1, D_MODEL)
    streams = [(jnp.concatenate([xp, xs[:n_a * s2]], axis=0), ((b1, s1), (n_a, s2)))]
    if n_a < b2:
        streams.append((xs[n_a * s2:], ((b2 - n_a, s2), (0, s2))))
    for l in range(DEPTH):
        lw = _layer_weights(l, *small)
        streams = [(layer(x2, lw, l, grp), grp) for x2, grp in streams]
    outs = [x2 for x2, _ in streams]
    y_sample = jnp.concatenate([outs[0][t1:]] + outs[1:], axis=0)
    return (outs[0][:t1].reshape(x_prompt.shape), y_sample.reshape(x_sample.shape))
```
